```python
import math
import jax, jax.numpy as jnp
from jax import lax
import numpy as np

D_MODEL = 2048
BATCH = 4
SEQ = 2048
DEPTH = 4

BRANCH_WIDTH = 3 * D_MODEL // 4
N_BRANCHES = 3
GMLP_CHUNK = 128
GMLP_GROUP_DIM = 128
GMLP_GROUPS = BRANCH_WIDTH // GMLP_GROUP_DIM
ATT_HEAD_DIM = 128
ATT_HEADS_PER_GROUP = 4
DILATED_GROUPS = ((128, 1), (512, 4), (2048, 16))
ATT_N_GROUPS = 3
ATT_HEADS = ATT_N_GROUPS * ATT_HEADS_PER_GROUP
ATT_QKV_WIDTH = ATT_HEADS * ATT_HEAD_DIM
ATT_OUT_WIDTH = ATT_HEADS_PER_GROUP * ATT_HEAD_DIM
ATT_BLOCK = 128
ROPE_THETA = 10000.0
SSM_D_INNER = BRANCH_WIDTH
SSM_HEAD_DIM = 64
SSM_HEADS = SSM_D_INNER // SSM_HEAD_DIM
SSM_GROUPS = 4
SSM_D_STATE = 128
SSM_CONV = 4
SSM_CHUNK = 128
SSM_CONV_DIM = SSM_D_INNER + 2 * SSM_GROUPS * SSM_D_STATE
IN_SIZES = (BRANCH_WIDTH, BRANCH_WIDTH, BRANCH_WIDTH,
            ATT_QKV_WIDTH, ATT_QKV_WIDTH, ATT_QKV_WIDTH, ATT_OUT_WIDTH,
            SSM_D_INNER, SSM_CONV_DIM, SSM_HEADS,
            D_MODEL, D_MODEL, D_MODEL)
N_IN = sum(IN_SIZES)

kernel_name = 'hybrid_gated_parallel_mixer'


def _rmsnorm(x, w, eps=1e-6):
    xf = x.astype(jnp.float32)
    y = xf * lax.rsqrt(jnp.mean(xf * xf, axis=-1, keepdims=True) + eps)
    return (y * w.astype(jnp.float32)).astype(x.dtype)


def _layernorm(x, w, b, eps=1e-5):
    xf = x.astype(jnp.float32)
    mu = jnp.mean(xf, axis=-1, keepdims=True)
    var = jnp.mean(jnp.square(xf - mu), axis=-1, keepdims=True)
    y = (xf - mu) * lax.rsqrt(var + eps)
    return (y * w.astype(jnp.float32) + b.astype(jnp.float32)).astype(x.dtype)


def _rope_tables(s):
    inv = 1.0 / (ROPE_THETA ** (jnp.arange(0, ATT_HEAD_DIM, 2, dtype=jnp.float32) / ATT_HEAD_DIM))
    ang = jnp.arange(s, dtype=jnp.float32)[:, None] * inv[None, :]
    return jnp.cos(ang), jnp.sin(ang)


def _rope(t, cos, sin):
    tf = t.astype(jnp.float32)
    t1, t2 = jnp.split(tf, 2, axis=-1)
    c = cos[None, :, None, :]
    s = sin[None, :, None, :]
    return jnp.concatenate([t1 * c - t2 * s, t2 * c + t1 * s], axis=-1).astype(t.dtype)


def _gmlp_chunked(u, v, zg, ln_w, ln_b, ws, bs):
    u = jax.nn.gelu(u)
    v = _layernorm(jax.nn.gelu(v), ln_w, ln_b)
    b, s, _ = v.shape
    vc = v.reshape(b, s // GMLP_CHUNK, GMLP_CHUNK, GMLP_GROUPS, GMLP_GROUP_DIM)
    causal = jnp.tril(jnp.ones((GMLP_CHUNK, GMLP_CHUNK), dtype=bool))
    wm = jnp.where(causal[None], ws, jnp.zeros_like(ws))
    mixed = jnp.einsum('gts,bcsge->bctge', wm, vc) + bs.T[None, None, :, :, None]
    mixed = mixed.reshape(b, s, BRANCH_WIDTH)
    return u * mixed * jax.nn.silu(zg)


def _banded_causal_attention(q, k, v, steps):
    n, lp, h, e = q.shape
    nb = lp // ATT_BLOCK
    qb = q.reshape(n, nb, ATT_BLOCK, h, e)
    kb = k.reshape(n, nb, ATT_BLOCK, h, e)
    vb = v.reshape(n, nb, ATT_BLOCK, h, e)

    def with_prev(t):
        prev = jnp.pad(t, ((0, 0), (1, 0), (0, 0), (0, 0), (0, 0)))[:, :-1]
        return jnp.concatenate([prev, t], axis=2)

    kc, vc = with_prev(kb), with_prev(vb)
    scale = 1.0 / math.sqrt(e)
    sc = jnp.einsum('nbqhe,nbkhe->nbhqk', qb, kc, preferred_element_type=jnp.float32) * scale
    qi = jnp.arange(ATT_BLOCK)[:, None] + ATT_BLOCK
    ki = jnp.arange(2 * ATT_BLOCK)[None, :]
    dist = qi - ki
    band = (dist >= 0) & (dist <= steps)
    blk = jnp.arange(nb)[:, None, None]
    exists = (blk * ATT_BLOCK + ki[None] - ATT_BLOCK) >= 0
    mask = band[None] & exists
    sc = jnp.where(mask[None, :, None], sc, -jnp.inf)
    lse = jax.nn.logsumexp(sc, axis=-1)
    p = jnp.exp(sc - lse[..., None])
    o = jnp.einsum('nbhqk,nbkhe->nbqhe', p.astype(v.dtype), vc).reshape(n, lp, h, e)
    lse = lse.transpose(0, 1, 3, 2).reshape(n, lp, h)
    return o, lse


def _dilated_group(q, k, v, window, dilation):
    b, s, h, e = q.shape
    l = s // dilation
    lp = -(-l // ATT_BLOCK) * ATT_BLOCK

    def to_res(t):
        t = t.reshape(b, l, dilation, h, e).transpose(0, 2, 1, 3, 4).reshape(b * dilation, l, h, e)
        return jnp.pad(t, ((0, 0), (0, lp - l), (0, 0), (0, 0)))

    o, lse = _banded_causal_attention(to_res(q), to_res(k), to_res(v), window // dilation)
    o = o[:, :l].reshape(b, dilation, l, h, e).transpose(0, 2, 1, 3, 4).reshape(b, s, h, e)
    lse = lse[:, :l].reshape(b, dilation, l, h).transpose(0, 2, 1, 3).reshape(b, s, h)
    return o, lse


def _dilated_attention(q, k, v, zg, cos, sin):
    b, s, _ = q.shape
    q = _rope(q.reshape(b, s, ATT_HEADS, ATT_HEAD_DIM), cos, sin)
    k = _rope(k.reshape(b, s, ATT_HEADS, ATT_HEAD_DIM), cos, sin)
    v = v.reshape(b, s, ATT_HEADS, ATT_HEAD_DIM)
    outs, lses = [], []
    for gi, (window, dil) in enumerate(DILATED_GROUPS):
        sl = slice(gi * ATT_HEADS_PER_GROUP, (gi + 1) * ATT_HEADS_PER_GROUP)
        o, lse = _dilated_group(q[:, :, sl], k[:, :, sl], v[:, :, sl], window, dil)
        outs.append(o)
        lses.append(lse)
    wts = jax.nn.softmax(jnp.stack(lses, axis=0), axis=0)
    o = jnp.sum(wts[..., None] * jnp.stack(outs, axis=0).astype(jnp.float32), axis=0)
    o = o.astype(zg.dtype).reshape(b, s, ATT_OUT_WIDTH)
    return o * jax.nn.silu(zg)


def _ssd_chunked(x, dt, a, bm, cm):
    b, s, h, p = x.shape
    g, n = bm.shape[2], bm.shape[3]
    j = h // g
    c = s // SSM_CHUNK
    l = SSM_CHUNK
    xdt = (x * dt[..., None]).reshape(b, c, l, g, j, p)
    bm = bm.reshape(b, c, l, g, n)
    cm = cm.reshape(b, c, l, g, n)
    da = (dt * a).reshape(b, c, l, g, j).transpose(0, 3, 4, 1, 2)
    cum = jnp.cumsum(da, axis=-1)
    seg = cum[..., :, None] - cum[..., None, :]
    causal = jnp.tril(jnp.ones((l, l), dtype=bool))
    decay = jnp.exp(jnp.where(causal, seg, -jnp.inf))
    cb = jnp.einsum('bctgn,bcsgn->bgcts', cm, bm)
    y_diag = jnp.einsum('bgcts,bgjcts,bcsgjp->bctgjp', cb, decay, xdt)
    decay_states = jnp.exp(cum[..., -1:] - cum)
    states = jnp.einsum('bclgn,bgjcl,bclgjp->bcgjpn', bm, decay_states, xdt)
    chunk_decay = jnp.exp(cum[..., -1])

    def step(h_prev, inp):
        st, dec = inp
        return h_prev * dec[..., None, None] + st, h_prev

    init = jnp.zeros((b, g, j, p, n), dtype=jnp.float32)
    _, prev = lax.scan(step, init, (jnp.moveaxis(states, 1, 0), jnp.moveaxis(chunk_decay, -1, 0)))
    prev = jnp.moveaxis(prev, 0, 1)
    y_off = jnp.einsum('bctgn,bcgjpn,bgjct->bctgjp', cm, prev, jnp.exp(cum))
    return (y_diag + y_off).reshape(b, s, h, p)


def _mamba2(zc, xbc, dt_raw, conv_w, conv_b, dt_bias, a_log, d_skip, norm_w):
    b, s, _ = xbc.shape
    xbc = lax.conv_general_dilated(xbc, conv_w[:, None, :], window_strides=(1,),
                                   padding=[(SSM_CONV - 1, 0)],
                                   dimension_numbers=('NWC', 'WIO', 'NWC'),
                                   feature_group_count=SSM_CONV_DIM) + conv_b
    xbc = jax.nn.silu(xbc).astype(jnp.float32)
    xs, bm, cm = jnp.split(xbc, [SSM_D_INNER, SSM_D_INNER + SSM_GROUPS * SSM_D_STATE], axis=-1)
    dt = jax.nn.softplus(dt_raw.astype(jnp.float32) + dt_bias.astype(jnp.float32))
    a = -jnp.exp(a_log.astype(jnp.float32))
    xh = xs.reshape(b, s, SSM_HEADS, SSM_HEAD_DIM)
    y = _ssd_chunked(xh, dt, a, bm.reshape(b, s, SSM_GROUPS, SSM_D_STATE),
                     cm.reshape(b, s, SSM_GROUPS, SSM_D_STATE))
    y = y + d_skip.astype(jnp.float32)[:, None] * xh
    y = y.reshape(b, s, SSM_D_INNER) * jax.nn.silu(zc.astype(jnp.float32))
    yg = y.reshape(b, s, SSM_GROUPS, SSM_D_INNER // SSM_GROUPS)
    yg = yg * lax.rsqrt(jnp.mean(yg * yg, axis=-1, keepdims=True) + 1e-5)
    y = yg.reshape(b, s, SSM_D_INNER) * norm_w.astype(jnp.float32)
    return y.astype(zc.dtype)


def _layer(x, norm_w, w_in, ln_w, ln_b, ws, bs, conv_w, conv_b, dt_bias, a_log, d_skip,
           ssm_norm_w, wa, wb, wc, w_out, cos, sin):
    h = _rmsnorm(x, norm_w)
    proj = h @ w_in
    offs = [int(o) for o in np.cumsum(IN_SIZES)[:-1]]
    (ua, va, za, qb, kb, vb, zb, zc, xbc, dtc, gra, grb, grc) = jnp.split(proj, offs, axis=-1)
    ya = _gmlp_chunked(ua, va, za, ln_w, ln_b, ws, bs)
    yb = _dilated_attention(qb, kb, vb, zb, cos, sin)
    yc = _mamba2(zc, xbc, dtc, conv_w, conv_b, dt_bias, a_log, d_skip, ssm_norm_w)
    merged = (jax.nn.sigmoid(gra) * (ya @ wa)
              + jax.nn.sigmoid(grb) * (yb @ wb)
              + jax.nn.sigmoid(grc) * (yc @ wc))
    return x + merged @ w_out


def setup_inputs(seed: int = 0) -> dict:
    key = jax.random.key(seed)
    ks = jax.random.split(key, 20)
    f32 = jnp.float32
    nrm = lambda k, shape, sc: jax.random.normal(k, shape, f32) * sc
    dt_u = jax.random.uniform(ks[9], (DEPTH, SSM_HEADS), f32)
    dt0 = jnp.exp(dt_u * (math.log(0.1) - math.log(0.001)) + math.log(0.001))
    return {
        'x': nrm(ks[0], (BATCH, SEQ, D_MODEL), 1.0),
        'norm_w': 1.0 + nrm(ks[1], (DEPTH, D_MODEL), 0.02),
        'w_in': nrm(ks[2], (DEPTH, D_MODEL, N_IN), D_MODEL ** -0.5),
        'gmlp_ln_w': 1.0 + nrm(ks[3], (DEPTH, BRANCH_WIDTH), 0.02),
        'gmlp_ln_b': nrm(ks[4], (DEPTH, BRANCH_WIDTH), 0.02),
        'gmlp_ws': nrm(ks[5], (DEPTH, GMLP_GROUPS, GMLP_CHUNK, GMLP_CHUNK), 0.5 * GMLP_CHUNK ** -0.5),
        'gmlp_bs': 1.0 + nrm(ks[6], (DEPTH, GMLP_GROUPS, GMLP_CHUNK), 0.02),
        'conv_w': nrm(ks[7], (DEPTH, SSM_CONV, SSM_CONV_DIM), SSM_CONV ** -0.5),
        'conv_b': nrm(ks[8], (DEPTH, SSM_CONV_DIM), 0.02),
        'dt_bias': dt0 + jnp.log(-jnp.expm1(-dt0)),
        'a_log': jnp.log(jax.random.uniform(ks[10], (DEPTH, SSM_HEADS), f32, 1.0, 16.0)),
        'd_skip': 1.0 + nrm(ks[11], (DEPTH, SSM_HEADS), 0.02),
        'ssm_norm_w': 1.0 + nrm(ks[12], (DEPTH, SSM_D_INNER), 0.02),
        'w_branch_a': nrm(ks[13], (DEPTH, BRANCH_WIDTH, D_MODEL), BRANCH_WIDTH ** -0.5),
        'w_branch_b': nrm(ks[14], (DEPTH, ATT_OUT_WIDTH, D_MODEL), ATT_OUT_WIDTH ** -0.5),
        'w_branch_c': nrm(ks[15], (DEPTH, SSM_D_INNER, D_MODEL), SSM_D_INNER ** -0.5),
        'w_out': nrm(ks[16], (DEPTH, D_MODEL, D_MODEL), 0.5 * D_MODEL ** -0.5),
        'final_norm_w': 1.0 + nrm(ks[17], (D_MODEL,), 0.02),
    }


def reference(x, norm_w, w_in, gmlp_ln_w, gmlp_ln_b, gmlp_ws, gmlp_bs, conv_w, conv_b, dt_bias,
              a_log, d_skip, ssm_norm_w, w_branch_a, w_branch_b, w_branch_c, w_out, final_norm_w):
    cos, sin = _rope_tables(x.shape[1])
    for i in range(DEPTH):
        x = _layer(x, norm_w[i], w_in[i], gmlp_ln_w[i], gmlp_ln_b[i], gmlp_ws[i], gmlp_bs[i],
                   conv_w[i], conv_b[i], dt_bias[i], a_log[i], d_skip[i], ssm_norm_w[i],
                   w_branch_a[i], w_branch_b[i], w_branch_c[i], w_out[i], cos, sin)
    return _rmsnorm(x, final_norm_w)
```

```python
import functools
import math

import jax
import jax.numpy as jnp
import numpy as np
from jax import lax
from jax.experimental import pallas as pl
from jax.experimental.pallas import tpu as pltpu

D_MODEL = 2048
SEQ = 2048
DEPTH = 4
BRANCH_WIDTH = 1536
GMLP_CHUNK = 128
GMLP_GROUPS = 12
HEAD_DIM = 128
HEADS_PER_GROUP = 4
DILATED_GROUPS = ((128, 1), (512, 4), (2048, 16))
ATT_BLOCK = 128
ATT_OUT_WIDTH = HEADS_PER_GROUP * HEAD_DIM
ROPE_THETA = 10000.0
SSM_D_INNER = 1536
SSM_HEAD_DIM = 64
SSM_HEADS = 24
SSM_GROUPS = 4
SSM_D_STATE = 128
SSM_CONV = 4
SSM_CHUNK = 128
SSM_CONV_DIM = SSM_D_INNER + 2 * SSM_GROUPS * SSM_D_STATE
HEAD_PAIRS = SSM_HEADS // 2
PAIRS_PER_GROUP = HEAD_PAIRS // SSM_GROUPS

OFF_A = 0
OFF_B = 3 * BRANCH_WIDTH
OFF_C = OFF_B + 3 * BRANCH_WIDTH + ATT_OUT_WIDTH
OFF_DT = OFF_C + SSM_D_INNER + SSM_CONV_DIM
OFF_GATE = OFF_DT + SSM_HEADS
WIDTH_A = OFF_B - OFF_A
WIDTH_B = OFF_C - OFF_B
PROJ_TILE_N = 512
WIDTH_C = 4608
DT_COL = SSM_D_INNER + SSM_CONV_DIM

LANES = 128
VMEM_LIMIT = 56 * 1024 * 1024

F32 = jnp.float32
BF16 = jnp.bfloat16


def _params(*semantics):
    return pltpu.CompilerParams(dimension_semantics=semantics, vmem_limit_bytes=VMEM_LIMIT)


def _rmsnorm_kernel(x_ref, w_ref, o_ref):
    x = x_ref[...]
    ms = jnp.mean(x * x, axis=-1, keepdims=True)
    o_ref[...] = (x * lax.rsqrt(ms + 1e-6) * w_ref[...]).astype(o_ref.dtype)


def _rmsnorm(x, w, out_dtype, rows=256):
    m, d = x.shape
    return pl.pallas_call(
        _rmsnorm_kernel,
        grid=(m // rows,),
        in_specs=[pl.BlockSpec((rows, d), lambda i: (i, 0)),
                  pl.BlockSpec((1, d), lambda i: (0, 0))],
        out_specs=pl.BlockSpec((rows, d), lambda i: (i, 0)),
        out_shape=jax.ShapeDtypeStruct((m, d), out_dtype),
        compiler_params=_params("parallel"),
        name="rmsnorm",
    )(x, w.reshape(1, d))


def _matmul_kernel(a_ref, w_ref, o_ref):
    o_ref[...] = jnp.dot(a_ref[...], w_ref[...].astype(BF16), preferred_element_type=F32)


def _matmul_residual_kernel(a_ref, w_ref, r_ref, o_ref):
    o_ref[...] = r_ref[...] + jnp.dot(a_ref[...], w_ref[...].astype(BF16),
                                      preferred_element_type=F32)


def _matmul(a, w, col_off, n_cols, tm, tn, residual=None, name="matmul"):
    m, k = a.shape
    assert col_off % tn == 0 and n_cols % tn == 0 and m % tm == 0
    off = col_off // tn
    in_specs = [pl.BlockSpec((tm, k), lambda i, j: (i, 0)),
                pl.BlockSpec((k, tn), lambda i, j: (0, j + off))]
    args = [a, w]
    kern = _matmul_kernel
    if residual is not None:
        in_specs.append(pl.BlockSpec((tm, tn), lambda i, j: (i, j)))
        args.append(residual)
        kern = _matmul_residual_kernel
    return pl.pallas_call(
        kern,
        grid=(m // tm, n_cols // tn),
        in_specs=in_specs,
        out_specs=pl.BlockSpec((tm, tn), lambda i, j: (i, j)),
        out_shape=jax.ShapeDtypeStruct((m, n_cols), F32),
        compiler_params=_params("parallel", "arbitrary"),
        name=name,
    )(*args)


def _gmlp_kernel(u_ref, v_ref, z_ref, lnw_ref, lnb_ref, ws_ref, bst_ref, o_ref):
    v = jax.nn.gelu(v_ref[...])
    mu = jnp.mean(v, axis=-1, keepdims=True)
    vc = v - mu
    var = jnp.mean(vc * vc, axis=-1, keepdims=True)
    vn = vc * lax.rsqrt(var + 1e-5) * lnw_ref[...] + lnb_ref[...]
    row = lax.broadcasted_iota(jnp.int32, (GMLP_CHUNK, GMLP_CHUNK), 0)
    col = lax.broadcasted_iota(jnp.int32, (GMLP_CHUNK, GMLP_CHUNK), 1)
    causal = col <= row
    for g in range(GMLP_GROUPS):
        sl = slice(g * LANES, (g + 1) * LANES)
        wm = jnp.where(causal, ws_ref[g], 0.0).astype(BF16)
        mixed = jnp.dot(wm, vn[:, sl].astype(BF16), preferred_element_type=F32)
        mixed = mixed + bst_ref[:, g:g + 1]
        u = jax.nn.gelu(u_ref[:, sl])
        o_ref[:, sl] = (u * mixed * jax.nn.silu(z_ref[:, sl])).astype(o_ref.dtype)


def _gmlp(pa, ln_w, ln_b, ws, bs):
    m = pa.shape[0]
    w = BRANCH_WIDTH
    const2 = lambda i: (0, 0)
    return pl.pallas_call(
        _gmlp_kernel,
        grid=(m // GMLP_CHUNK,),
        in_specs=[pl.BlockSpec((GMLP_CHUNK, w), lambda i: (i, 0)),
                  pl.BlockSpec((GMLP_CHUNK, w), lambda i: (i, 1)),
                  pl.BlockSpec((GMLP_CHUNK, w), lambda i: (i, 2)),
                  pl.BlockSpec((1, w), const2),
                  pl.BlockSpec((1, w), const2),
                  pl.BlockSpec((GMLP_GROUPS, GMLP_CHUNK, GMLP_CHUNK), lambda i: (0, 0, 0)),
                  pl.BlockSpec((GMLP_CHUNK, GMLP_GROUPS), const2)],
        out_specs=pl.BlockSpec((GMLP_CHUNK, w), lambda i: (i, 0)),
        out_shape=jax.ShapeDtypeStruct((m, w), BF16),
        compiler_params=_params("parallel"),
        name="gmlp",
    )(pa, pa, pa, ln_w.reshape(1, w), ln_b.reshape(1, w), ws, bs.T)


ROPE_ROWS = 256


def _attn_kernel(q1, q2, q3, k1, k2, k3, v1, v2, v3, z_ref, cos_ref, sin_ref, o_ref,
                 qs1, qs2, qs3, ks1, ks2, ks3, os1, os2, os3, ls1, ls2, ls3):
    scale = 1.0 / math.sqrt(HEAD_DIM)

    def rope_step(i, carry):
        rows = pl.ds(pl.multiple_of(i * ROPE_ROWS, ROPE_ROWS), ROPE_ROWS)
        c = cos_ref[rows, :]
        s = sin_ref[rows, :]
        for src, dst, mul in ((q1, qs1, scale), (q2, qs2, scale), (q3, qs3, scale),
                              (k1, ks1, 1.0), (k2, ks2, 1.0), (k3, ks3, 1.0)):
            t = src[rows, :]
            r = t * c + pltpu.roll(t, HEAD_DIM // 2, 1) * s
            dst[rows, :] = r * mul if mul != 1.0 else r
        return carry

    lax.fori_loop(0, SEQ // ROPE_ROWS, rope_step, 0)

    row = lax.broadcasted_iota(jnp.int32, (ATT_BLOCK, ATT_BLOCK), 0)
    col = lax.broadcasted_iota(jnp.int32, (ATT_BLOCK, ATT_BLOCK), 1)
    lower = col <= row
    upper = col >= row
    nt = (((1,), (1,)), ((), ()))

    def run_group(qs, ks, v_ref, os, ls, dil):
        nb = SEQ // dil // ATT_BLOCK

        def rows_at(start):
            if dil == 1:
                return pl.ds(pl.multiple_of(start, ATT_BLOCK), ATT_BLOCK)
            return pl.ds(start, ATT_BLOCK, stride=dil)

        def body(idx, carry):
            r = idx % dil
            bi = idx // dil
            cur = rows_at(bi * (ATT_BLOCK * dil) + r)
            q = qs[cur, :].astype(BF16)
            s_cur = lax.dot_general(q, ks[cur, :].astype(BF16), nt, preferred_element_type=F32)
            s_cur = jnp.where(lower, s_cur, -jnp.inf)
            m = jnp.max(s_cur, axis=-1, keepdims=True)
            if nb > 1:
                prev = rows_at(jnp.maximum(bi - 1, 0) * (ATT_BLOCK * dil) + r)
                s_prev = lax.dot_general(q, ks[prev, :].astype(BF16), nt,
                                         preferred_element_type=F32)
                s_prev = jnp.where(jnp.logical_and(upper, bi > 0), s_prev, -jnp.inf)
                m = jnp.maximum(m, jnp.max(s_prev, axis=-1, keepdims=True))
            p_cur = jnp.exp(s_cur - m)
            l = jnp.sum(p_cur, axis=-1, keepdims=True)
            acc = jnp.dot(p_cur.astype(BF16), v_ref[cur, :].astype(BF16),
                          preferred_element_type=F32)
            if nb > 1:
                p_prev = jnp.exp(s_prev - m)
                l = l + jnp.sum(p_prev, axis=-1, keepdims=True)
                acc = acc + jnp.dot(p_prev.astype(BF16), v_ref[prev, :].astype(BF16),
                                    preferred_element_type=F32)
            os[cur, :] = acc / l
            ls[cur, :] = jnp.broadcast_to(m + jnp.log(l), (ATT_BLOCK, HEAD_DIM))
            return carry

        lax.fori_loop(0, nb * dil, body, 0)

    run_group(qs1, ks1, v1, os1, ls1, DILATED_GROUPS[0][1])
    run_group(qs2, ks2, v2, os2, ls2, DILATED_GROUPS[1][1])
    run_group(qs3, ks3, v3, os3, ls3, DILATED_GROUPS[2][1])

    def combine_step(i, carry):
        rows = pl.ds(pl.multiple_of(i * ROPE_ROWS, ROPE_ROWS), ROPE_ROWS)
        l1 = ls1[rows, :]
        l2 = ls2[rows, :]
        l3 = ls3[rows, :]
        mx = jnp.maximum(jnp.maximum(l1, l2), l3)
        e1 = jnp.exp(l1 - mx)
        e2 = jnp.exp(l2 - mx)
        e3 = jnp.exp(l3 - mx)
        o = (e1 * os1[rows, :] + e2 * os2[rows, :] + e3 * os3[rows, :]) / (e1 + e2 + e3)
        o_ref[rows, :] = (o * jax.nn.silu(z_ref[rows, :])).astype(o_ref.dtype)
        return carry

    lax.fori_loop(0, SEQ // ROPE_ROWS, combine_step, 0)


def _rope_tables():
    inv = 1.0 / (ROPE_THETA ** (jnp.arange(0, HEAD_DIM, 2, dtype=F32) / HEAD_DIM))
    ang = jnp.arange(SEQ, dtype=F32)[:, None] * inv[None, :]
    cos, sin = jnp.cos(ang), jnp.sin(ang)
    return jnp.concatenate([cos, cos], axis=-1), jnp.concatenate([-sin, sin], axis=-1)


def _attention(pb, cos2, sin2):
    m = pb.shape[0]
    batch = m // SEQ
    hpg = HEADS_PER_GROUP
    n_qkv = 3 * hpg

    def head_spec(seg, grp):
        return pl.BlockSpec((SEQ, HEAD_DIM), lambda b, j: (b, seg * n_qkv + grp * hpg + j))

    in_specs = [head_spec(seg, grp) for seg in range(3) for grp in range(3)]
    in_specs.append(pl.BlockSpec((SEQ, HEAD_DIM), lambda b, j: (b, 3 * n_qkv + j)))
    in_specs += [pl.BlockSpec((SEQ, HEAD_DIM), lambda b, j: (0, 0))] * 2
    return pl.pallas_call(
        _attn_kernel,
        grid=(batch, hpg),
        in_specs=in_specs,
        out_specs=pl.BlockSpec((SEQ, HEAD_DIM), lambda b, j: (b, j)),
        out_shape=jax.ShapeDtypeStruct((m, ATT_OUT_WIDTH), BF16),
        scratch_shapes=[pltpu.VMEM((SEQ, HEAD_DIM), F32)] * 12,
        compiler_params=_params("parallel", "parallel"),
        name="dilated_attention",
    )(*([pb] * 10), cos2, sin2)


CONV_HALO = 8


def _ssd_kernel(p_ref, convw_ref, convb_ref, dtb_ref, alog_ref, dskip_ref, normw_ref, expand_ref,
                o_ref, xext, xcs, ys, hstate):
    c = pl.program_id(1)
    L = SSM_CHUNK
    hi = lax.Precision.HIGHEST

    @pl.when(c == 0)
    def _():
        hstate[...] = jnp.zeros_like(hstate)
        xext[0:CONV_HALO, :] = jnp.zeros((CONV_HALO, SSM_CONV_DIM), F32)

    xext[CONV_HALO:CONV_HALO + L, :] = p_ref[:, SSM_D_INNER:SSM_D_INNER + SSM_CONV_DIM]
    for t in range(SSM_CONV_DIM // LANES):
        sl = slice(t * LANES, (t + 1) * LANES)
        acc = jnp.broadcast_to(convb_ref[:, sl], (L, LANES))
        for k in range(SSM_CONV):
            lo = CONV_HALO - (SSM_CONV - 1) + k
            acc = acc + convw_ref[k:k + 1, sl] * xext[lo:lo + L, sl]
        xcs[:, sl] = jax.nn.silu(acc)
    xext[0:CONV_HALO, :] = xext[L:L + CONV_HALO, :]

    lane = lax.broadcasted_iota(jnp.int32, (L, LANES), 1)
    row = lax.broadcasted_iota(jnp.int32, (L, L), 0)
    col = lax.broadcasted_iota(jnp.int32, (L, L), 1)
    causal = col <= row
    head_ok = lane < SSM_HEADS

    dt = jnp.where(head_ok, jax.nn.softplus(p_ref[:, DT_COL:DT_COL + LANES] + dtb_ref[...]), 0.0)
    da = dt * (-jnp.exp(alog_ref[...]))
    cum = jnp.dot(causal.astype(F32), da, precision=hi, preferred_element_type=F32)
    cum_t = cum.T
    cum_last = cum[L - 1:L, :]
    expcum = jnp.exp(cum)
    dsdt = jnp.exp(cum_last - cum) * dt
    expand = expand_ref[...]
    dt_e = jnp.dot(dt, expand, precision=hi, preferred_element_type=F32)
    dsdt_e = jnp.dot(dsdt, expand, precision=hi, preferred_element_type=F32)
    expcum_e = jnp.dot(expcum, expand, precision=hi, preferred_element_type=F32)

    first_head = lax.broadcasted_iota(jnp.int32, (L, LANES), 1) < SSM_HEAD_DIM
    for g in range(SSM_GROUPS):
        b_off = SSM_D_INNER + g * SSM_D_STATE
        c_off = SSM_D_INNER + SSM_GROUPS * SSM_D_STATE + g * SSM_D_STATE
        bm_t = xcs[:, b_off:b_off + SSM_D_STATE].T.astype(BF16)
        cm = xcs[:, c_off:c_off + SSM_D_STATE].astype(BF16)
        cb = jnp.where(causal, jnp.dot(cm, bm_t, preferred_element_type=F32), 0.0)
        for pj in range(PAIRS_PER_GROUP):
            pair = g * PAIRS_PER_GROUP + pj
            sl = slice(pair * LANES, (pair + 1) * LANES)
            xs = xcs[:, sl]
            xdt = (xs * dt_e[:, sl]).astype(BF16)
            xw = (xs * dsdt_e[:, sl]).astype(BF16)
            y_heads = []
            for hh in range(2):
                h = 2 * pair + hh
                seg = cum[:, h:h + 1] - cum_t[h:h + 1, :]
                decay = jnp.exp(jnp.where(causal, seg, -jnp.inf))
                y_heads.append(jnp.dot((cb * decay).astype(BF16), xdt,
                                       preferred_element_type=F32))
            y = jnp.where(first_head, y_heads[0], y_heads[1])
            h_prev = hstate[pair]
            y = y + jnp.dot(cm, h_prev.astype(BF16), preferred_element_type=F32) * expcum_e[:, sl]
            ys[:, sl] = y + dskip_ref[:, sl] * xs
            st = jnp.dot(bm_t, xw, preferred_element_type=F32)
            hstate[pair] = h_prev * expcum_e[L - 1:L, sl] + st

    gw = SSM_D_INNER // SSM_GROUPS
    for g in range(SSM_GROUPS):
        sl = slice(g * gw, (g + 1) * gw)
        y = ys[:, sl] * jax.nn.silu(p_ref[:, sl])
        ms = jnp.mean(y * y, axis=-1, keepdims=True)
        o_ref[:, sl] = (y * lax.rsqrt(ms + 1e-5) * normw_ref[:, sl]).astype(o_ref.dtype)


def _head_expand_matrix():
    e = np.zeros((LANES, SSM_D_INNER), np.float32)
    for h in range(SSM_HEADS):
        e[h, h * SSM_HEAD_DIM:(h + 1) * SSM_HEAD_DIM] = 1.0
    return jnp.asarray(e)


def _pad_lanes(v):
    return jnp.pad(v, (0, LANES - v.shape[0])).reshape(1, LANES)


def _ssd(pc, conv_w, conv_b, dt_bias, a_log, d_skip, norm_w):
    m = pc.shape[0]
    batch = m // SEQ
    nc = SEQ // SSM_CHUNK
    const2 = lambda b, c: (0, 0)
    return pl.pallas_call(
        _ssd_kernel,
        grid=(batch, nc),
        in_specs=[pl.BlockSpec((SSM_CHUNK, WIDTH_C), lambda b, c: (b * nc + c, 0)),
                  pl.BlockSpec((SSM_CONV, SSM_CONV_DIM), const2),
                  pl.BlockSpec((1, SSM_CONV_DIM), const2),
                  pl.BlockSpec((1, LANES), const2),
                  pl.BlockSpec((1, LANES), const2),
                  pl.BlockSpec((1, SSM_D_INNER), const2),
                  pl.BlockSpec((1, SSM_D_INNER), const2),
                  pl.BlockSpec((LANES, SSM_D_INNER), const2)],
        out_specs=pl.BlockSpec((SSM_CHUNK, SSM_D_INNER), lambda b, c: (b * nc + c, 0)),
        out_shape=jax.ShapeDtypeStruct((m, SSM_D_INNER), BF16),
        scratch_shapes=[pltpu.VMEM((CONV_HALO + SSM_CHUNK, SSM_CONV_DIM), F32),
                        pltpu.VMEM((SSM_CHUNK, SSM_CONV_DIM), F32),
                        pltpu.VMEM((SSM_CHUNK, SSM_D_INNER), F32),
                        pltpu.VMEM((HEAD_PAIRS, SSM_D_STATE, LANES), F32)],
        compiler_params=_params("parallel", "arbitrary"),
        name="ssd",
    )(pc, conv_w, conv_b.reshape(1, -1), _pad_lanes(dt_bias), _pad_lanes(a_log),
      jnp.repeat(d_skip, SSM_HEAD_DIM).reshape(1, -1), norm_w.reshape(1, -1),
      _head_expand_matrix())


def _merge_kernel(ya_ref, yb_ref, yc_ref, ga_ref, gb_ref, gc_ref, wa_ref, wb_ref, wc_ref, o_ref):
    def branch(y_ref, g_ref, w_ref):
        return jax.nn.sigmoid(g_ref[...]) * jnp.dot(y_ref[...], w_ref[...].astype(BF16),
                                                    preferred_element_type=F32)
    acc = branch(ya_ref, ga_ref, wa_ref) + branch(yb_ref, gb_ref, wb_ref)
    o_ref[...] = (acc + branch(yc_ref, gc_ref, wc_ref)).astype(o_ref.dtype)


def _merge(ya, yb, yc, gates, wa, wb, wc, tm=1024, tn=256):
    m = ya.shape[0]
    nj = D_MODEL // tn
    y_spec = lambda w: pl.BlockSpec((tm, w), lambda i, j: (i, 0))
    g_spec = lambda k: pl.BlockSpec((tm, tn), lambda i, j: (i, j + k * nj))
    w_spec = lambda w: pl.BlockSpec((w, tn), lambda i, j: (0, j))
    return pl.pallas_call(
        _merge_kernel,
        grid=(m // tm, nj),
        in_specs=[y_spec(BRANCH_WIDTH), y_spec(ATT_OUT_WIDTH), y_spec(SSM_D_INNER),
                  g_spec(0), g_spec(1), g_spec(2),
                  w_spec(BRANCH_WIDTH), w_spec(ATT_OUT_WIDTH), w_spec(SSM_D_INNER)],
        out_specs=pl.BlockSpec((tm, tn), lambda i, j: (i, j)),
        out_shape=jax.ShapeDtypeStruct((m, D_MODEL), BF16),
        compiler_params=_params("parallel", "arbitrary"),
        name="merge",
    )(ya, yb, yc, gates, gates, gates, wa, wb, wc)


def _layer(x, norm_w, w_in, ln_w, ln_b, ws, bs, conv_w, conv_b, dt_bias, a_log, d_skip,
           ssm_norm_w, wa, wb, wc, w_out, cos2, sin2):
    h = _rmsnorm(x, norm_w, BF16)
    tm, tn = 2048, PROJ_TILE_N
    pa = _matmul(h, w_in, OFF_A, WIDTH_A, tm, tn, name="proj_a")
    pb = _matmul(h, w_in, OFF_B, WIDTH_B, tm, tn, name="proj_b")
    pc = _matmul(h, w_in, OFF_C, WIDTH_C, tm, tn, name="proj_c")
    gates = _matmul(h, w_in[:, OFF_GATE:], 0, 3 * D_MODEL, tm, tn, name="proj_gates")
    ya = _gmlp(pa, ln_w, ln_b, ws, bs)
    yb = _attention(pb, cos2, sin2)
    yc = _ssd(pc, conv_w, conv_b, dt_bias, a_log, d_skip, ssm_norm_w)
    merged = _merge(ya, yb, yc, gates, wa, wb, wc)
    return _matmul(merged, w_out, 0, D_MODEL, 1024, 512, residual=x, name="out_proj")


def kernel(x, norm_w, w_in, gmlp_ln_w, gmlp_ln_b, gmlp_ws, gmlp_bs, conv_w, conv_b, dt_bias,
           a_log, d_skip, ssm_norm_w, w_branch_a, w_branch_b, w_branch_c, w_out, final_norm_w):
    b, s, d = x.shape
    assert (s, d) == (SEQ, D_MODEL)
    cos2, sin2 = _rope_tables()
    xf = x.reshape(b * s, d)
    for i in range(DEPTH):
        xf = _layer(xf, norm_w[i], w_in[i], gmlp_ln_w[i], gmlp_ln_b[i], gmlp_ws[i], gmlp_bs[i],
                    conv_w[i], conv_b[i], dt_bias[i], a_log[i], d_skip[i], ssm_norm_w[i],
                    w_branch_a[i], w_branch_b[i], w_branch_c[i], w_out[i], cos2, sin2)
    return _rmsnorm(xf, final_norm_w, x.dtype).reshape(b, s, d)
```

```python
import math

import jax
import jax.numpy as jnp
import numpy as np
from jax import lax
from jax.experimental import pallas as pl
from jax.experimental.pallas import tpu as pltpu

D_MODEL = 2048
SEQ = 2048
DEPTH = 4
BRANCH_WIDTH = 1536
GMLP_CHUNK = 128
GMLP_GROUPS = 12
HEAD_DIM = 128
HEADS_PER_GROUP = 4
DILATED_GROUPS = ((128, 1), (512, 4), (2048, 16))
ATT_BLOCK = 128
ATT_OUT_WIDTH = HEADS_PER_GROUP * HEAD_DIM
ROPE_THETA = 10000.0
SSM_D_INNER = 1536
SSM_HEAD_DIM = 64
SSM_HEADS = 24
SSM_GROUPS = 4
SSM_D_STATE = 128
SSM_CONV = 4
SSM_CHUNK = 128
SSM_CONV_DIM = SSM_D_INNER + 2 * SSM_GROUPS * SSM_D_STATE
HEAD_PAIRS = SSM_HEADS // 2
PAIRS_PER_GROUP = HEAD_PAIRS // SSM_GROUPS

OFF_A = 0
OFF_B = 3 * BRANCH_WIDTH
OFF_C = OFF_B + 3 * BRANCH_WIDTH + ATT_OUT_WIDTH
OFF_DT = OFF_C + SSM_D_INNER + SSM_CONV_DIM
OFF_GATE = OFF_DT + SSM_HEADS
WIDTH_A = OFF_B - OFF_A
WIDTH_B = OFF_C - OFF_B
WIDTH_C = 4608
DT_COL = SSM_D_INNER + SSM_CONV_DIM

LANES = 128
LOG2_E = 1.4426950408889634
LN_2 = 0.6931471805599453
VMEM_LIMIT = 56 * 1024 * 1024
PROJ_TILE_M, PROJ_TILE_N = 2048, 512
OUT_TILE_M, OUT_TILE_N = 1024, 512
MERGE_TILE_M, MERGE_TILE_N = 1024, 256

F32 = jnp.float32
BF16 = jnp.bfloat16


def _params(*semantics):
    return pltpu.CompilerParams(dimension_semantics=semantics, vmem_limit_bytes=VMEM_LIMIT)


def _layer_rows(layer, width, n_grid):
    if n_grid == 1:
        return pl.BlockSpec((None, 1, width), lambda i: (layer, 0, 0))
    return pl.BlockSpec((None, 1, width), lambda i, j: (layer, 0, 0))


def _rmsnorm_kernel(x_ref, w_ref, o_ref):
    x = x_ref[...]
    ms = jnp.mean(x * x, axis=-1, keepdims=True)
    o_ref[...] = (x * lax.rsqrt(ms + 1e-6) * w_ref[...]).astype(o_ref.dtype)


def _rmsnorm(x, w, layer, out_dtype, rows=256):
    m, d = x.shape
    return pl.pallas_call(
        _rmsnorm_kernel,
        grid=(m // rows,),
        in_specs=[pl.BlockSpec((rows, d), lambda i: (i, 0)), _layer_rows(layer, d, 1)],
        out_specs=pl.BlockSpec((rows, d), lambda i: (i, 0)),
        out_shape=jax.ShapeDtypeStruct((m, d), out_dtype),
        compiler_params=_params("parallel"),
        name="rmsnorm",
    )(x, w)


def _matmul_kernel(a_ref, w_ref, o_ref):
    o_ref[...] = jnp.dot(a_ref[...], w_ref[...].astype(BF16), preferred_element_type=F32)


def _matmul_residual_kernel(a_ref, w_ref, r_ref, o_ref):
    o_ref[...] = r_ref[...] + jnp.dot(a_ref[...], w_ref[...].astype(BF16),
                                      preferred_element_type=F32)


def _matmul(a, w, layer, col_off, n_cols, tm, tn, residual=None, name="matmul"):
    m, k = a.shape
    assert col_off % tn == 0 and n_cols % tn == 0 and m % tm == 0
    off = col_off // tn
    in_specs = [pl.BlockSpec((tm, k), lambda i, j: (i, 0)),
                pl.BlockSpec((None, k, tn), lambda i, j: (layer, 0, j + off))]
    args = [a, w]
    kern = _matmul_kernel
    if residual is not None:
        in_specs.append(pl.BlockSpec((tm, tn), lambda i, j: (i, j)))
        args.append(residual)
        kern = _matmul_residual_kernel
    return pl.pallas_call(
        kern,
        grid=(m // tm, n_cols // tn),
        in_specs=in_specs,
        out_specs=pl.BlockSpec((tm, tn), lambda i, j: (i, j)),
        out_shape=jax.ShapeDtypeStruct((m, n_cols), F32),
        compiler_params=_params("parallel", "arbitrary"),
        name=name,
    )(*args)


def _gmlp_kernel(u_ref, v_ref, z_ref, lnw_ref, lnb_ref, ws_ref, bst_ref, o_ref):
    v = jax.nn.gelu(v_ref[...])
    mu = jnp.mean(v, axis=-1, keepdims=True)
    vc = v - mu
    var = jnp.mean(vc * vc, axis=-1, keepdims=True)
    vn = vc * lax.rsqrt(var + 1e-5) * lnw_ref[...] + lnb_ref[...]
    row = lax.broadcasted_iota(jnp.int32, (GMLP_CHUNK, GMLP_CHUNK), 0)
    col = lax.broadcasted_iota(jnp.int32, (GMLP_CHUNK, GMLP_CHUNK), 1)
    causal = col <= row
    for g in range(GMLP_GROUPS):
        sl = slice(g * LANES, (g + 1) * LANES)
        wm = jnp.where(causal, ws_ref[g], 0.0).astype(BF16)
        mixed = jnp.dot(wm, vn[:, sl].astype(BF16), preferred_element_type=F32)
        mixed = mixed + bst_ref[:, g:g + 1]
        u = jax.nn.gelu(u_ref[:, sl])
        o_ref[:, sl] = (u * mixed * jax.nn.silu(z_ref[:, sl])).astype(o_ref.dtype)


def _gmlp(pa, ln_w, ln_b, ws, bs_t, layer):
    m = pa.shape[0]
    w = BRANCH_WIDTH
    return pl.pallas_call(
        _gmlp_kernel,
        grid=(m // GMLP_CHUNK,),
        in_specs=[pl.BlockSpec((GMLP_CHUNK, w), lambda i: (i, 0)),
                  pl.BlockSpec((GMLP_CHUNK, w), lambda i: (i, 1)),
                  pl.BlockSpec((GMLP_CHUNK, w), lambda i: (i, 2)),
                  _layer_rows(layer, w, 1),
                  _layer_rows(layer, w, 1),
                  pl.BlockSpec((None, GMLP_GROUPS, GMLP_CHUNK, GMLP_CHUNK),
                               lambda i: (layer, 0, 0, 0)),
                  pl.BlockSpec((None, GMLP_CHUNK, GMLP_GROUPS), lambda i: (layer, 0, 0))],
        out_specs=pl.BlockSpec((GMLP_CHUNK, w), lambda i: (i, 0)),
        out_shape=jax.ShapeDtypeStruct((m, w), BF16),
        compiler_params=_params("parallel"),
        name="gmlp",
    )(pa, pa, pa, ln_w, ln_b, ws, bs_t)


ROPE_ROWS = 256
ATT_UNROLL = 8


def _attn_kernel(q1, q2, q3, k1, k2, k3, v1, v2, v3, z_ref, cos_ref, sin_ref, o_ref,
                 qs1, qs2, qs3, ks1, ks2, ks3, os1, os2, os3, ls1, ls2, ls3,
                 sc1, sc2, sc3, mx1, mx2, mx3):
    scale = LOG2_E / math.sqrt(HEAD_DIM)

    def rope_step(i, carry):
        rows = pl.ds(pl.multiple_of(i * ROPE_ROWS, ROPE_ROWS), ROPE_ROWS)
        c = cos_ref[rows, :]
        s = sin_ref[rows, :]
        for src, dst, mul in ((q1, qs1, scale), (q2, qs2, scale), (q3, qs3, scale),
                              (k1, ks1, None), (k2, ks2, None), (k3, ks3, None)):
            t = src[rows, :]
            r = t * c + pltpu.roll(t, HEAD_DIM // 2, 1) * s
            dst[rows, :] = r if mul is None else r * mul
        return carry

    lax.fori_loop(0, SEQ // ROPE_ROWS, rope_step, 0)

    row = lax.broadcasted_iota(jnp.int32, (ATT_BLOCK, ATT_BLOCK), 0)
    col = lax.broadcasted_iota(jnp.int32, (ATT_BLOCK, ATT_BLOCK), 1)
    lower = col <= row
    upper = col >= row
    nt = (((1,), (1,)), ((), ()))
    ones = jnp.ones((ATT_BLOCK, HEAD_DIM), BF16)

    def with_ones(v):
        return jnp.concatenate([v.astype(BF16), ones], axis=1)

    def block_rows(dil, idx):
        def rows_at(start):
            if dil == 1:
                return pl.ds(pl.multiple_of(start, ATT_BLOCK), ATT_BLOCK)
            return pl.ds(start, ATT_BLOCK, stride=dil)

        r = idx % dil
        bi = idx // dil
        cur = rows_at(bi * (ATT_BLOCK * dil) + r)
        prev = rows_at(jnp.maximum(bi - 1, 0) * (ATT_BLOCK * dil) + r)
        return cur, prev, bi > 0

    def scores(qs, ks, sc, mx, dil, idx):
        cur, prev, has_prev = block_rows(dil, idx)
        q = qs[cur, :].astype(BF16)
        s_cur = lax.dot_general(q, ks[cur, :].astype(BF16), nt, preferred_element_type=F32)
        s_cur = jnp.where(lower, s_cur, -jnp.inf)
        sc[idx, :, 0:ATT_BLOCK] = s_cur
        if SEQ // dil > ATT_BLOCK:
            s_prev = lax.dot_general(q, ks[prev, :].astype(BF16), nt,
                                     preferred_element_type=F32)
            s_prev = jnp.where(jnp.logical_and(upper, has_prev), s_prev, -jnp.inf)
            sc[idx, :, ATT_BLOCK:2 * ATT_BLOCK] = s_prev
            s_cur = jnp.maximum(s_cur, s_prev)
        mx[idx] = jnp.broadcast_to(jnp.max(s_cur, axis=-1, keepdims=True), (ATT_BLOCK, HEAD_DIM))

    def values(v_ref, sc, mx, os, ls, dil, idx):
        cur, prev, _ = block_rows(dil, idx)
        m = mx[idx]
        acc = jnp.dot(jnp.exp2(sc[idx, :, 0:ATT_BLOCK] - m).astype(BF16),
                      with_ones(v_ref[cur, :]), preferred_element_type=F32)
        if SEQ // dil > ATT_BLOCK:
            acc = acc + jnp.dot(jnp.exp2(sc[idx, :, ATT_BLOCK:2 * ATT_BLOCK] - m).astype(BF16),
                                with_ones(v_ref[prev, :]), preferred_element_type=F32)
        l = acc[:, HEAD_DIM:]
        os[cur, :] = acc[:, :HEAD_DIM] / l
        ls[cur, :] = m * LN_2 + jnp.log(l)

    groups = ((qs1, ks1, v1, os1, ls1, sc1, mx1, DILATED_GROUPS[0][1]),
              (qs2, ks2, v2, os2, ls2, sc2, mx2, DILATED_GROUPS[1][1]),
              (qs3, ks3, v3, os3, ls3, sc3, mx3, DILATED_GROUPS[2][1]))
    n_idx = SEQ // ATT_BLOCK

    def scores_step(it, carry):
        for u in range(ATT_UNROLL):
            for qs, ks, _, _, _, sc, mx, dil in groups:
                scores(qs, ks, sc, mx, dil, it * ATT_UNROLL + u)
        return carry

    def values_step(it, carry):
        for u in range(ATT_UNROLL):
            for _, _, v_ref, os, ls, sc, mx, dil in groups:
                values(v_ref, sc, mx, os, ls, dil, it * ATT_UNROLL + u)
        return carry

    lax.fori_loop(0, n_idx // ATT_UNROLL, scores_step, 0)
    lax.fori_loop(0, n_idx // ATT_UNROLL, values_step, 0)

    def combine_step(i, carry):
        rows = pl.ds(pl.multiple_of(i * ROPE_ROWS, ROPE_ROWS), ROPE_ROWS)
        l1 = ls1[rows, :]
        l2 = ls2[rows, :]
        l3 = ls3[rows, :]
        mx = jnp.maximum(jnp.maximum(l1, l2), l3)
        e1 = jnp.exp(l1 - mx)
        e2 = jnp.exp(l2 - mx)
        e3 = jnp.exp(l3 - mx)
        o = (e1 * os1[rows, :] + e2 * os2[rows, :] + e3 * os3[rows, :]) / (e1 + e2 + e3)
        o_ref[rows, :] = (o * jax.nn.silu(z_ref[rows, :])).astype(o_ref.dtype)
        return carry

    lax.fori_loop(0, SEQ // ROPE_ROWS, combine_step, 0)


def _rope_tables():
    inv = 1.0 / (ROPE_THETA ** (jnp.arange(0, HEAD_DIM, 2, dtype=F32) / HEAD_DIM))
    ang = jnp.arange(SEQ, dtype=F32)[:, None] * inv[None, :]
    cos, sin = jnp.cos(ang), jnp.sin(ang)
    return jnp.concatenate([cos, cos], axis=-1), jnp.concatenate([-sin, sin], axis=-1)


def _attention(pb, cos2, sin2):
    m = pb.shape[0]
    batch = m // SEQ
    hpg = HEADS_PER_GROUP
    n_qkv = 3 * hpg
    n_blocks = SEQ // ATT_BLOCK

    def head_spec(seg, grp):
        return pl.BlockSpec((SEQ, HEAD_DIM), lambda b, j: (b, seg * n_qkv + grp * hpg + j))

    in_specs = [head_spec(seg, grp) for seg in range(3) for grp in range(3)]
    in_specs.append(pl.BlockSpec((SEQ, HEAD_DIM), lambda b, j: (b, 3 * n_qkv + j)))
    in_specs += [pl.BlockSpec((SEQ, HEAD_DIM), lambda b, j: (0, 0))] * 2
    return pl.pallas_call(
        _attn_kernel,
        grid=(batch, hpg),
        in_specs=in_specs,
        out_specs=pl.BlockSpec((SEQ, HEAD_DIM), lambda b, j: (b, j)),
        out_shape=jax.ShapeDtypeStruct((m, ATT_OUT_WIDTH), BF16),
        scratch_shapes=([pltpu.VMEM((SEQ, HEAD_DIM), F32)] * 12
                        + [pltpu.VMEM((n_blocks, ATT_BLOCK, 2 * ATT_BLOCK), F32)] * 2
                        + [pltpu.VMEM((n_blocks, ATT_BLOCK, ATT_BLOCK), F32)] * 4),
        compiler_params=_params("parallel", "parallel"),
        name="dilated_attention",
    )(*([pb] * 10), cos2, sin2)


CONV_HALO = 8
CONV_TILES = SSM_CONV_DIM // LANES


def _ssd_kernel(p_ref, convw_ref, convb_ref, dtb_ref, alog_ref, dskip_ref, normw_ref,
                o_ref, xext, xcs, ys, hstate):
    c = pl.program_id(1)
    L = SSM_CHUNK

    @pl.when(c == 0)
    def _():
        hstate[...] = jnp.zeros_like(hstate)
        xext[:, 0:CONV_HALO, :] = jnp.zeros((CONV_TILES, CONV_HALO, LANES), F32)

    for t in range(CONV_TILES):
        sl = slice(t * LANES, (t + 1) * LANES)
        xext[t, CONV_HALO:CONV_HALO + L, :] = p_ref[:, SSM_D_INNER + t * LANES:
                                                    SSM_D_INNER + (t + 1) * LANES]
        acc = jnp.broadcast_to(convb_ref[:, sl], (L, LANES))
        for k in range(SSM_CONV):
            lo = CONV_HALO - (SSM_CONV - 1) + k
            acc = acc + convw_ref[k:k + 1, sl] * xext[t, lo:lo + L, :]
        xcs[:, sl] = jax.nn.silu(acc)
        xext[t, 0:CONV_HALO, :] = xext[t, L:L + CONV_HALO, :]

    lane = lax.broadcasted_iota(jnp.int32, (L, LANES), 1)
    row = lax.broadcasted_iota(jnp.int32, (L, L), 0)
    col = lax.broadcasted_iota(jnp.int32, (L, L), 1)
    causal = col <= row
    first_head = lane < SSM_HEAD_DIM

    dt = jnp.where(lane < SSM_HEADS,
                   jax.nn.softplus(p_ref[:, DT_COL:DT_COL + LANES] + dtb_ref[...]), 0.0)
    da = dt * (-jnp.exp(alog_ref[...]))
    cum = jnp.dot(causal.astype(F32), da, precision=lax.Precision.HIGHEST,
                  preferred_element_type=F32)
    cum = cum * LOG2_E
    cum_t = cum.T
    dt_t = dt.T
    dsdt_t = jnp.exp2(cum_t[:, L - 1:L] - cum_t) * dt_t
    shift_t = cum_t - jnp.log2(dt_t)

    for g in range(SSM_GROUPS):
        b_off = SSM_D_INNER + g * SSM_D_STATE
        c_off = SSM_D_INNER + SSM_GROUPS * SSM_D_STATE + g * SSM_D_STATE
        bm_t = xcs[:, b_off:b_off + SSM_D_STATE].T
        cm = xcs[:, c_off:c_off + SSM_D_STATE].astype(BF16)
        cb = jnp.where(causal, jnp.dot(cm, bm_t.astype(BF16), preferred_element_type=F32), 0.0)
        for pj in range(PAIRS_PER_GROUP):
            pair = g * PAIRS_PER_GROUP + pj
            sl = slice(pair * LANES, (pair + 1) * LANES)
            xs = xcs[:, sl]
            xs_b = xs.astype(BF16)
            y_h, st_h, cc_h = [], [], []
            for hh in range(2):
                h = 2 * pair + hh
                ccol = jnp.broadcast_to(cum[:, h:h + 1], (L, L))
                decay_dt = jnp.exp2(jnp.where(causal, ccol - shift_t[h:h + 1, :], -jnp.inf))
                y_h.append(jnp.dot((cb * decay_dt).astype(BF16), xs_b,
                                   preferred_element_type=F32))
                bw = (bm_t * dsdt_t[h:h + 1, :]).astype(BF16)
                st_h.append(jnp.dot(bw, xs_b, preferred_element_type=F32))
                cc_h.append(ccol)
            expcum = jnp.exp2(jnp.where(first_head, cc_h[0], cc_h[1]))
            h_prev = hstate[pair]
            y = jnp.where(first_head, y_h[0], y_h[1])
            y = y + jnp.dot(cm, h_prev.astype(BF16), preferred_element_type=F32) * expcum
            ys[:, sl] = y + dskip_ref[:, sl] * xs
            hstate[pair] = (h_prev * expcum[L - 1:L, :]
                            + jnp.where(first_head, st_h[0], st_h[1]))

    gw = SSM_D_INNER // SSM_GROUPS
    for g in range(SSM_GROUPS):
        sl = slice(g * gw, (g + 1) * gw)
        y = ys[:, sl] * jax.nn.silu(p_ref[:, sl])
        ms = jnp.mean(y * y, axis=-1, keepdims=True)
        o_ref[:, sl] = (y * lax.rsqrt(ms + 1e-5) * normw_ref[:, sl]).astype(o_ref.dtype)


def _pad_lanes(v):
    return jnp.pad(v, ((0, 0), (0, LANES - v.shape[1])))[:, None, :]


def _ssd(pc, conv_w, conv_b, dt_bias, a_log, d_skip_e, norm_w, layer):
    m = pc.shape[0]
    batch = m // SEQ
    nc = SEQ // SSM_CHUNK
    return pl.pallas_call(
        _ssd_kernel,
        grid=(batch, nc),
        in_specs=[pl.BlockSpec((SSM_CHUNK, WIDTH_C), lambda b, c: (b * nc + c, 0)),
                  pl.BlockSpec((None, SSM_CONV, SSM_CONV_DIM), lambda b, c: (layer, 0, 0)),
                  _layer_rows(layer, SSM_CONV_DIM, 2),
                  _layer_rows(layer, LANES, 2),
                  _layer_rows(layer, LANES, 2),
                  _layer_rows(layer, SSM_D_INNER, 2),
                  _layer_rows(layer, SSM_D_INNER, 2)],
        out_specs=pl.BlockSpec((SSM_CHUNK, SSM_D_INNER), lambda b, c: (b * nc + c, 0)),
        out_shape=jax.ShapeDtypeStruct((m, SSM_D_INNER), BF16),
        scratch_shapes=[pltpu.VMEM((CONV_TILES, CONV_HALO + SSM_CHUNK, LANES), F32),
                        pltpu.VMEM((SSM_CHUNK, SSM_CONV_DIM), F32),
                        pltpu.VMEM((SSM_CHUNK, SSM_D_INNER), F32),
                        pltpu.VMEM((HEAD_PAIRS, SSM_D_STATE, LANES), F32)],
        compiler_params=_params("parallel", "arbitrary"),
        name="ssd",
    )(pc, conv_w, conv_b, dt_bias, a_log, d_skip_e, norm_w)


def _merge_kernel(ya_ref, yb_ref, yc_ref, ga_ref, gb_ref, gc_ref, wa_ref, wb_ref, wc_ref, o_ref):
    def branch(y_ref, g_ref, w_ref):
        return jax.nn.sigmoid(g_ref[...]) * jnp.dot(y_ref[...], w_ref[...].astype(BF16),
                                                    preferred_element_type=F32)
    acc = branch(ya_ref, ga_ref, wa_ref) + branch(yb_ref, gb_ref, wb_ref)
    o_ref[...] = (acc + branch(yc_ref, gc_ref, wc_ref)).astype(o_ref.dtype)


def _merge(ya, yb, yc, gates, wa, wb, wc, layer):
    m = ya.shape[0]
    tm, tn = MERGE_TILE_M, MERGE_TILE_N
    nj = D_MODEL // tn
    y_spec = lambda w: pl.BlockSpec((tm, w), lambda i, j: (i, 0))
    g_spec = lambda k: pl.BlockSpec((tm, tn), lambda i, j: (i, j + k * nj))
    w_spec = lambda w: pl.BlockSpec((None, w, tn), lambda i, j: (layer, 0, j))
    return pl.pallas_call(
        _merge_kernel,
        grid=(m // tm, nj),
        in_specs=[y_spec(BRANCH_WIDTH), y_spec(ATT_OUT_WIDTH), y_spec(SSM_D_INNER),
                  g_spec(0), g_spec(1), g_spec(2),
                  w_spec(BRANCH_WIDTH), w_spec(ATT_OUT_WIDTH), w_spec(SSM_D_INNER)],
        out_specs=pl.BlockSpec((tm, tn), lambda i, j: (i, j)),
        out_shape=jax.ShapeDtypeStruct((m, D_MODEL), BF16),
        compiler_params=_params("parallel", "arbitrary"),
        name="merge",
    )(ya, yb, yc, gates, gates, gates, wa, wb, wc)


def kernel(x, norm_w, w_in, gmlp_ln_w, gmlp_ln_b, gmlp_ws, gmlp_bs, conv_w, conv_b, dt_bias,
           a_log, d_skip, ssm_norm_w, w_branch_a, w_branch_b, w_branch_c, w_out, final_norm_w):
    b, s, d = x.shape
    assert (s, d) == (SEQ, D_MODEL)
    cos2, sin2 = _rope_tables()
    rows = lambda p: p[:, None, :]
    norm_w3, ln_w3, ln_b3 = rows(norm_w), rows(gmlp_ln_w), rows(gmlp_ln_b)
    conv_b3, ssm_norm_w3 = rows(conv_b), rows(ssm_norm_w)
    bs_t = jnp.swapaxes(gmlp_bs, 1, 2)
    dt_bias3, a_log3 = _pad_lanes(dt_bias), _pad_lanes(a_log)
    d_skip_e = rows(jnp.repeat(d_skip, SSM_HEAD_DIM, axis=1))
    w_gate = w_in[:, :, OFF_GATE:]
    xf = x.reshape(b * s, d)
    tm, tn = PROJ_TILE_M, PROJ_TILE_N
    for i in range(DEPTH):
        h = _rmsnorm(xf, norm_w3, i, BF16)
        pa = _matmul(h, w_in, i, OFF_A, WIDTH_A, tm, tn, name="proj_a")
        pb = _matmul(h, w_in, i, OFF_B, WIDTH_B, tm, tn, name="proj_b")
        pc = _matmul(h, w_in, i, OFF_C, WIDTH_C, tm, tn, name="proj_c")
        gates = _matmul(h, w_gate, i, 0, 3 * D_MODEL, tm, tn, name="proj_gates")
        ya = _gmlp(pa, ln_w3, ln_b3, gmlp_ws, bs_t, i)
        yb = _attention(pb, cos2, sin2)
        yc = _ssd(pc, conv_w, conv_b3, dt_bias3, a_log3, d_skip_e, ssm_norm_w3, i)
        merged = _merge(ya, yb, yc, gates, w_branch_a, w_branch_b, w_branch_c, i)
        xf = _matmul(merged, w_out, i, 0, D_MODEL, OUT_TILE_M, OUT_TILE_N, residual=xf,
                     name="out_proj")
    return _rmsnorm(xf, final_norm_w.reshape(1, 1, d), 0, x.dtype).reshape(b, s, d)
```

```python
import math

import jax
import jax.numpy as jnp
import numpy as np
from jax import lax
from jax.experimental import pallas as pl
from jax.experimental.pallas import tpu as pltpu

D_MODEL = 2048
SEQ = 2048
DEPTH = 4
BRANCH_WIDTH = 1536
GMLP_CHUNK = 128
GMLP_GROUPS = 12
HEAD_DIM = 128
HEADS_PER_GROUP = 4
DILATED_GROUPS = ((128, 1), (512, 4), (2048, 16))
ATT_BLOCK = 128
ATT_OUT_WIDTH = HEADS_PER_GROUP * HEAD_DIM
ROPE_THETA = 10000.0
SSM_D_INNER = 1536
SSM_HEAD_DIM = 64
SSM_HEADS = 24
SSM_GROUPS = 4
SSM_D_STATE = 128
SSM_CONV = 4
SSM_CHUNK = 128
SSM_CONV_DIM = SSM_D_INNER + 2 * SSM_GROUPS * SSM_D_STATE
HEAD_PAIRS = SSM_HEADS // 2
PAIRS_PER_GROUP = HEAD_PAIRS // SSM_GROUPS

OFF_A = 0
OFF_B = 3 * BRANCH_WIDTH
OFF_C = OFF_B + 3 * BRANCH_WIDTH + ATT_OUT_WIDTH
OFF_DT = OFF_C + SSM_D_INNER + SSM_CONV_DIM
OFF_GATE = OFF_DT + SSM_HEADS
WIDTH_A = OFF_B - OFF_A
WIDTH_B = OFF_C - OFF_B
WIDTH_C = 4608
DT_COL = SSM_D_INNER + SSM_CONV_DIM

LANES = 128
SUBLANES = 8
LOG2_E = 1.4426950408889634
LN_2 = 0.6931471805599453
VMEM_LIMIT = 56 * 1024 * 1024
PROJ_TILE_M, PROJ_TILE_N = 2048, 512
OUT_TILE_M, OUT_TILE_N = 1024, 512
MERGE_TILE_M, MERGE_TILE_N = 1024, 256

F32 = jnp.float32
BF16 = jnp.bfloat16


def _params(*semantics):
    return pltpu.CompilerParams(dimension_semantics=semantics, vmem_limit_bytes=VMEM_LIMIT)


def _layer_rows(layer, width, n_grid):
    if n_grid == 1:
        return pl.BlockSpec((None, 1, width), lambda i: (layer, 0, 0))
    return pl.BlockSpec((None, 1, width), lambda i, j: (layer, 0, 0))


def _rmsnorm_kernel(x_ref, w_ref, o_ref):
    x = x_ref[...]
    ms = jnp.mean(x * x, axis=-1, keepdims=True)
    o_ref[...] = (x * lax.rsqrt(ms + 1e-6) * w_ref[...]).astype(o_ref.dtype)


def _rmsnorm(x, w, layer, out_dtype, rows=256):
    m, d = x.shape
    return pl.pallas_call(
        _rmsnorm_kernel,
        grid=(m // rows,),
        in_specs=[pl.BlockSpec((rows, d), lambda i: (i, 0)), _layer_rows(layer, d, 1)],
        out_specs=pl.BlockSpec((rows, d), lambda i: (i, 0)),
        out_shape=jax.ShapeDtypeStruct((m, d), out_dtype),
        compiler_params=_params("parallel"),
        name="rmsnorm",
    )(x, w)


def _matmul_kernel(a_ref, w_ref, o_ref):
    o_ref[...] = jnp.dot(a_ref[...], w_ref[...].astype(BF16), preferred_element_type=F32)


def _matmul_residual_kernel(a_ref, w_ref, r_ref, o_ref):
    o_ref[...] = r_ref[...] + jnp.dot(a_ref[...], w_ref[...].astype(BF16),
                                      preferred_element_type=F32)


def _matmul(a, w, layer, col_off, n_cols, tm, tn, residual=None, name="matmul"):
    m, k = a.shape
    assert col_off % tn == 0 and n_cols % tn == 0 and m % tm == 0
    off = col_off // tn
    in_specs = [pl.BlockSpec((tm, k), lambda i, j: (i, 0)),
                pl.BlockSpec((None, k, tn), lambda i, j: (layer, 0, j + off))]
    args = [a, w]
    kern = _matmul_kernel
    if residual is not None:
        in_specs.append(pl.BlockSpec((tm, tn), lambda i, j: (i, j)))
        args.append(residual)
        kern = _matmul_residual_kernel
    return pl.pallas_call(
        kern,
        grid=(m // tm, n_cols // tn),
        in_specs=in_specs,
        out_specs=pl.BlockSpec((tm, tn), lambda i, j: (i, j)),
        out_shape=jax.ShapeDtypeStruct((m, n_cols), F32),
        compiler_params=_params("parallel", "arbitrary"),
        name=name,
    )(*args)


NT_DIMS = (((1,), (1,)), ((), ()))


def _wt_rows_spec(layer, row_off, tn, k):
    assert row_off % SUBLANES == 0 and tn % SUBLANES == 0
    return pl.BlockSpec((pl.Element(1), pl.Element(tn), pl.Element(k)),
                        lambda i, j: (layer, pl.multiple_of(row_off + j * tn, SUBLANES), 0))


def _matmul_nt_kernel(a_ref, wt_ref, o_ref):
    o_ref[...] = lax.dot_general(a_ref[...], wt_ref[0].astype(BF16), NT_DIMS,
                                 preferred_element_type=F32)


def _matmul_nt(a, wt, layer, row_off, n_cols, tm, tn, name):
    m, k = a.shape
    assert n_cols % tn == 0 and m % tm == 0
    return pl.pallas_call(
        _matmul_nt_kernel,
        grid=(m // tm, n_cols // tn),
        in_specs=[pl.BlockSpec((tm, k), lambda i, j: (i, 0)), _wt_rows_spec(layer, row_off, tn, k)],
        out_specs=pl.BlockSpec((tm, tn), lambda i, j: (i, j)),
        out_shape=jax.ShapeDtypeStruct((m, n_cols), F32),
        compiler_params=_params("parallel", "arbitrary"),
        name=name,
    )(a, wt)


GMLP_STEP_ROWS = 2 * GMLP_CHUNK

def _gmlp_kernel(u_ref, v_ref, z_ref, lnw_ref, lnb_ref, ws_ref, bst_ref, o_ref):
    row = lax.broadcasted_iota(jnp.int32, (GMLP_CHUNK, GMLP_CHUNK), 0)
    col = lax.broadcasted_iota(jnp.int32, (GMLP_CHUNK, GMLP_CHUNK), 1)
    causal = col <= row
    for c in range(GMLP_STEP_ROWS // GMLP_CHUNK):
        rows = slice(c * GMLP_CHUNK, (c + 1) * GMLP_CHUNK)
        v = jax.nn.gelu(v_ref[rows, :])
        mu = jnp.mean(v, axis=-1, keepdims=True)
        vc = v - mu
        var = jnp.mean(vc * vc, axis=-1, keepdims=True)
        vn = vc * lax.rsqrt(var + 1e-5) * lnw_ref[...] + lnb_ref[...]
        for g in range(GMLP_GROUPS):
            sl = slice(g * LANES, (g + 1) * LANES)
            wm = jnp.where(causal, ws_ref[g], 0.0).astype(BF16)
            mixed = jnp.dot(wm, vn[:, sl].astype(BF16), preferred_element_type=F32)
            mixed = mixed + bst_ref[:, g:g + 1]
            u = jax.nn.gelu(u_ref[rows, sl])
            o_ref[rows, sl] = (u * mixed * jax.nn.silu(z_ref[rows, sl])).astype(o_ref.dtype)


def _gmlp(pa, ln_w, ln_b, ws, bs_t, layer):
    m = pa.shape[0]
    w = BRANCH_WIDTH
    return pl.pallas_call(
        _gmlp_kernel,
        grid=(m // GMLP_STEP_ROWS,),
        in_specs=[pl.BlockSpec((GMLP_STEP_ROWS, w), lambda i: (i, 0)),
                  pl.BlockSpec((GMLP_STEP_ROWS, w), lambda i: (i, 1)),
                  pl.BlockSpec((GMLP_STEP_ROWS, w), lambda i: (i, 2)),
                  _layer_rows(layer, w, 1),
                  _layer_rows(layer, w, 1),
                  pl.BlockSpec((None, GMLP_GROUPS, GMLP_CHUNK, GMLP_CHUNK),
                               lambda i: (layer, 0, 0, 0)),
                  pl.BlockSpec((None, GMLP_CHUNK, GMLP_GROUPS), lambda i: (layer, 0, 0))],
        out_specs=pl.BlockSpec((GMLP_STEP_ROWS, w), lambda i: (i, 0)),
        out_shape=jax.ShapeDtypeStruct((m, w), BF16),
        compiler_params=_params("parallel"),
        name="gmlp",
    )(pa, pa, pa, ln_w, ln_b, ws, bs_t)


ROPE_ROWS = 256
ATT_UNROLL = 8


def _attn_kernel(q1, q2, q3, k1, k2, k3, v1, v2, v3, z_ref, cos_ref, sin_ref, o_ref,
                 qs1, qs2, qs3, ks1, ks2, ks3, os1, os2, os3, ls1, ls2, ls3,
                 sc1, sc2, sc3, mx1, mx2, mx3):
    scale = LOG2_E / math.sqrt(HEAD_DIM)

    def rope_step(i, carry):
        rows = pl.ds(pl.multiple_of(i * ROPE_ROWS, ROPE_ROWS), ROPE_ROWS)
        c = cos_ref[rows, :]
        s = sin_ref[rows, :]
        for src, dst, mul in ((q1, qs1, scale), (q2, qs2, scale), (q3, qs3, scale),
                              (k1, ks1, None), (k2, ks2, None), (k3, ks3, None)):
            t = src[rows, :]
            r = t * c + pltpu.roll(t, HEAD_DIM // 2, 1) * s
            dst[rows, :] = r if mul is None else r * mul
        return carry

    lax.fori_loop(0, SEQ // ROPE_ROWS, rope_step, 0)

    row = lax.broadcasted_iota(jnp.int32, (ATT_BLOCK, ATT_BLOCK), 0)
    col = lax.broadcasted_iota(jnp.int32, (ATT_BLOCK, ATT_BLOCK), 1)
    lower = col <= row
    upper = col >= row
    nt = (((1,), (1,)), ((), ()))
    ones = jnp.ones((ATT_BLOCK, HEAD_DIM), BF16)

    def with_ones(v):
        return jnp.concatenate([v.astype(BF16), ones], axis=1)

    def block_rows(dil, idx):
        def rows_at(start):
            if dil == 1:
                return pl.ds(pl.multiple_of(start, ATT_BLOCK), ATT_BLOCK)
            return pl.ds(start, ATT_BLOCK, stride=dil)

        r = idx % dil
        bi = idx // dil
        cur = rows_at(bi * (ATT_BLOCK * dil) + r)
        prev = rows_at(jnp.maximum(bi - 1, 0) * (ATT_BLOCK * dil) + r)
        return cur, prev, bi > 0

    def scores(qs, ks, sc, mx, dil, idx):
        cur, prev, has_prev = block_rows(dil, idx)
        q = qs[cur, :].astype(BF16)
        s_cur = lax.dot_general(q, ks[cur, :].astype(BF16), nt, preferred_element_type=F32)
        s_cur = jnp.where(lower, s_cur, -jnp.inf)
        sc[idx, :, 0:ATT_BLOCK] = s_cur
        if SEQ // dil > ATT_BLOCK:
            s_prev = lax.dot_general(q, ks[prev, :].astype(BF16), nt,
                                     preferred_element_type=F32)
            s_prev = jnp.where(jnp.logical_and(upper, has_prev), s_prev, -jnp.inf)
            sc[idx, :, ATT_BLOCK:2 * ATT_BLOCK] = s_prev
            s_cur = jnp.maximum(s_cur, s_prev)
        mx[idx] = jnp.broadcast_to(jnp.max(s_cur, axis=-1, keepdims=True), (ATT_BLOCK, HEAD_DIM))

    def values(v_ref, sc, mx, os, ls, dil, idx):
        cur, prev, _ = block_rows(dil, idx)
        m = mx[idx]
        acc = jnp.dot(jnp.exp2(sc[idx, :, 0:ATT_BLOCK] - m).astype(BF16),
                      with_ones(v_ref[cur, :]), preferred_element_type=F32)
        if SEQ // dil > ATT_BLOCK:
            acc = acc + jnp.dot(jnp.exp2(sc[idx, :, ATT_BLOCK:2 * ATT_BLOCK] - m).astype(BF16),
                                with_ones(v_ref[prev, :]), preferred_element_type=F32)
        l = acc[:, HEAD_DIM:]
        os[cur, :] = acc[:, :HEAD_DIM] / l
        ls[cur, :] = m * LN_2 + jnp.log(l)

    groups = ((qs1, ks1, v1, os1, ls1, sc1, mx1, DILATED_GROUPS[0][1]),
              (qs2, ks2, v2, os2, ls2, sc2, mx2, DILATED_GROUPS[1][1]),
              (qs3, ks3, v3, os3, ls3, sc3, mx3, DILATED_GROUPS[2][1]))
    n_idx = SEQ // ATT_BLOCK

    def scores_step(it, carry):
        for u in range(ATT_UNROLL):
            for qs, ks, _, _, _, sc, mx, dil in groups:
                scores(qs, ks, sc, mx, dil, it * ATT_UNROLL + u)
        return carry

    def values_step(it, carry):
        for u in range(ATT_UNROLL):
            for _, _, v_ref, os, ls, sc, mx, dil in groups:
                values(v_ref, sc, mx, os, ls, dil, it * ATT_UNROLL + u)
        return carry

    lax.fori_loop(0, n_idx // ATT_UNROLL, scores_step, 0)
    lax.fori_loop(0, n_idx // ATT_UNROLL, values_step, 0)

    def combine_step(i, carry):
        rows = pl.ds(pl.multiple_of(i * ROPE_ROWS, ROPE_ROWS), ROPE_ROWS)
        l1 = ls1[rows, :]
        l2 = ls2[rows, :]
        l3 = ls3[rows, :]
        mx = jnp.maximum(jnp.maximum(l1, l2), l3)
        e1 = jnp.exp(l1 - mx)
        e2 = jnp.exp(l2 - mx)
        e3 = jnp.exp(l3 - mx)
        o = (e1 * os1[rows, :] + e2 * os2[rows, :] + e3 * os3[rows, :]) / (e1 + e2 + e3)
        o_ref[rows, :] = (o * jax.nn.silu(z_ref[rows, :])).astype(o_ref.dtype)
        return carry

    lax.fori_loop(0, SEQ // ROPE_ROWS, combine_step, 0)


def _rope_tables():
    inv = 1.0 / (ROPE_THETA ** (jnp.arange(0, HEAD_DIM, 2, dtype=F32) / HEAD_DIM))
    ang = jnp.arange(SEQ, dtype=F32)[:, None] * inv[None, :]
    cos, sin = jnp.cos(ang), jnp.sin(ang)
    return jnp.concatenate([cos, cos], axis=-1), jnp.concatenate([-sin, sin], axis=-1)


def _attention(pb, cos2, sin2):
    m = pb.shape[0]
    batch = m // SEQ
    hpg = HEADS_PER_GROUP
    n_qkv = 3 * hpg
    n_blocks = SEQ // ATT_BLOCK

    def head_spec(seg, grp):
        return pl.BlockSpec((SEQ, HEAD_DIM), lambda b, j: (b, seg * n_qkv + grp * hpg + j))

    in_specs = [head_spec(seg, grp) for seg in range(3) for grp in range(3)]
    in_specs.append(pl.BlockSpec((SEQ, HEAD_DIM), lambda b, j: (b, 3 * n_qkv + j)))
    in_specs += [pl.BlockSpec((SEQ, HEAD_DIM), lambda b, j: (0, 0))] * 2
    return pl.pallas_call(
        _attn_kernel,
        grid=(batch, hpg),
        in_specs=in_specs,
        out_specs=pl.BlockSpec((SEQ, HEAD_DIM), lambda b, j: (b, j)),
        out_shape=jax.ShapeDtypeStruct((m, ATT_OUT_WIDTH), BF16),
        scratch_shapes=([pltpu.VMEM((SEQ, HEAD_DIM), F32)] * 12
                        + [pltpu.VMEM((n_blocks, ATT_BLOCK, 2 * ATT_BLOCK), F32)] * 2
                        + [pltpu.VMEM((n_blocks, ATT_BLOCK, ATT_BLOCK), F32)] * 4),
        compiler_params=_params("parallel", "parallel"),
        name="dilated_attention",
    )(*([pb] * 10), cos2, sin2)


CONV_HALO = 8
CONV_TILES = SSM_CONV_DIM // LANES


def _ssd_kernel(p_ref, convw_ref, convb_ref, dtb_ref, alog_ref, dskip_ref, normw_ref,
                o_ref, xext, xcs, ys, hstate):
    c = pl.program_id(1)
    L = SSM_CHUNK

    @pl.when(c == 0)
    def _():
        hstate[...] = jnp.zeros_like(hstate)
        xext[:, 0:CONV_HALO, :] = jnp.zeros((CONV_TILES, CONV_HALO, LANES), F32)

    for t in range(CONV_TILES):
        sl = slice(t * LANES, (t + 1) * LANES)
        xext[t, CONV_HALO:CONV_HALO + L, :] = p_ref[:, SSM_D_INNER + t * LANES:
                                                    SSM_D_INNER + (t + 1) * LANES]
        acc = jnp.broadcast_to(convb_ref[:, sl], (L, LANES))
        for k in range(SSM_CONV):
            lo = CONV_HALO - (SSM_CONV - 1) + k
            acc = acc + convw_ref[k:k + 1, sl] * xext[t, lo:lo + L, :]
        xcs[:, sl] = jax.nn.silu(acc)
        xext[t, 0:CONV_HALO, :] = xext[t, L:L + CONV_HALO, :]

    lane = lax.broadcasted_iota(jnp.int32, (L, LANES), 1)
    row = lax.broadcasted_iota(jnp.int32, (L, L), 0)
    col = lax.broadcasted_iota(jnp.int32, (L, L), 1)
    causal = col <= row
    first_head = lane < SSM_HEAD_DIM

    dt = jnp.where(lane < SSM_HEADS,
                   jax.nn.softplus(p_ref[:, DT_COL:DT_COL + LANES] + dtb_ref[...]), 0.0)
    da = dt * (-jnp.exp(alog_ref[...]))
    cum = jnp.dot(causal.astype(F32), da, precision=lax.Precision.HIGHEST,
                  preferred_element_type=F32)
    cum = cum * LOG2_E
    cum_t = cum.T
    dt_t = dt.T
    dsdt_t = jnp.exp2(cum_t[:, L - 1:L] - cum_t) * dt_t
    shift_t = cum_t - jnp.log2(dt_t)

    for g in range(SSM_GROUPS):
        b_off = SSM_D_INNER + g * SSM_D_STATE
        c_off = SSM_D_INNER + SSM_GROUPS * SSM_D_STATE + g * SSM_D_STATE
        bm_t = xcs[:, b_off:b_off + SSM_D_STATE].T
        cm = xcs[:, c_off:c_off + SSM_D_STATE].astype(BF16)
        cb = jnp.where(causal, jnp.dot(cm, bm_t.astype(BF16), preferred_element_type=F32), 0.0)
        for pj in range(PAIRS_PER_GROUP):
            pair = g * PAIRS_PER_GROUP + pj
            sl = slice(pair * LANES, (pair + 1) * LANES)
            xs = xcs[:, sl]
            xs_b = xs.astype(BF16)
            y_h, st_h, cc_h = [], [], []
            for hh in range(2):
                h = 2 * pair + hh
                ccol = jnp.broadcast_to(cum[:, h:h + 1], (L, L))
                decay_dt = jnp.exp2(jnp.where(causal, ccol - shift_t[h:h + 1, :], -jnp.inf))
                y_h.append(jnp.dot((cb * decay_dt).astype(BF16), xs_b,
                                   preferred_element_type=F32))
                bw = (bm_t * dsdt_t[h:h + 1, :]).astype(BF16)
                st_h.append(jnp.dot(bw, xs_b, preferred_element_type=F32))
                cc_h.append(ccol)
            expcum = jnp.exp2(jnp.where(first_head, cc_h[0], cc_h[1]))
            h_prev = hstate[pair]
            y = jnp.where(first_head, y_h[0], y_h[1])
            y = y + jnp.dot(cm, h_prev.astype(BF16), preferred_element_type=F32) * expcum
            ys[:, sl] = y + dskip_ref[:, sl] * xs
            hstate[pair] = (h_prev * expcum[L - 1:L, :]
                            + jnp.where(first_head, st_h[0], st_h[1]))

    gw = SSM_D_INNER // SSM_GROUPS
    for g in range(SSM_GROUPS):
        sl = slice(g * gw, (g + 1) * gw)
        y = ys[:, sl] * jax.nn.silu(p_ref[:, sl])
        ms = jnp.mean(y * y, axis=-1, keepdims=True)
        o_ref[:, sl] = (y * lax.rsqrt(ms + 1e-5) * normw_ref[:, sl]).astype(o_ref.dtype)


def _pad_lanes(v):
    return jnp.pad(v, ((0, 0), (0, LANES - v.shape[1])))[:, None, :]


def _ssd(pc, conv_w, conv_b, dt_bias, a_log, d_skip_e, norm_w, layer):
    m = pc.shape[0]
    batch = m // SEQ
    nc = SEQ // SSM_CHUNK
    return pl.pallas_call(
        _ssd_kernel,
        grid=(batch, nc),
        in_specs=[pl.BlockSpec((SSM_CHUNK, WIDTH_C), lambda b, c: (b * nc + c, 0)),
                  pl.BlockSpec((None, SSM_CONV, SSM_CONV_DIM), lambda b, c: (layer, 0, 0)),
                  _layer_rows(layer, SSM_CONV_DIM, 2),
                  _layer_rows(layer, LANES, 2),
                  _layer_rows(layer, LANES, 2),
                  _layer_rows(layer, SSM_D_INNER, 2),
                  _layer_rows(layer, SSM_D_INNER, 2)],
        out_specs=pl.BlockSpec((SSM_CHUNK, SSM_D_INNER), lambda b, c: (b * nc + c, 0)),
        out_shape=jax.ShapeDtypeStruct((m, SSM_D_INNER), BF16),
        scratch_shapes=[pltpu.VMEM((CONV_TILES, CONV_HALO + SSM_CHUNK, LANES), F32),
                        pltpu.VMEM((SSM_CHUNK, SSM_CONV_DIM), F32),
                        pltpu.VMEM((SSM_CHUNK, SSM_D_INNER), F32),
                        pltpu.VMEM((HEAD_PAIRS, SSM_D_STATE, LANES), F32)],
        compiler_params=_params("parallel", "arbitrary"),
        name="ssd",
    )(pc, conv_w, conv_b, dt_bias, a_log, d_skip_e, norm_w)


def _merge_kernel(h_ref, ya_ref, yb_ref, yc_ref, ga_ref, gb_ref, gc_ref, wa_ref, wb_ref, wc_ref,
                  o_ref):
    h = h_ref[...]

    def branch(y_ref, g_ref, w_ref):
        gate = lax.dot_general(h, g_ref[0].astype(BF16), NT_DIMS, preferred_element_type=F32)
        return jax.nn.sigmoid(gate) * jnp.dot(y_ref[...], w_ref[...].astype(BF16),
                                              preferred_element_type=F32)
    acc = branch(ya_ref, ga_ref, wa_ref) + branch(yb_ref, gb_ref, wb_ref)
    o_ref[...] = (acc + branch(yc_ref, gc_ref, wc_ref)).astype(o_ref.dtype)


def _merge(h, ya, yb, yc, wt_in, wa, wb, wc, layer):
    m = ya.shape[0]
    tm, tn = MERGE_TILE_M, MERGE_TILE_N
    y_spec = lambda w: pl.BlockSpec((tm, w), lambda i, j: (i, 0))
    g_spec = lambda k: _wt_rows_spec(layer, OFF_GATE + k * D_MODEL, tn, D_MODEL)
    w_spec = lambda w: pl.BlockSpec((None, w, tn), lambda i, j: (layer, 0, j))
    return pl.pallas_call(
        _merge_kernel,
        grid=(m // tm, D_MODEL // tn),
        in_specs=[y_spec(D_MODEL), y_spec(BRANCH_WIDTH), y_spec(ATT_OUT_WIDTH),
                  y_spec(SSM_D_INNER), g_spec(0), g_spec(1), g_spec(2),
                  w_spec(BRANCH_WIDTH), w_spec(ATT_OUT_WIDTH), w_spec(SSM_D_INNER)],
        out_specs=pl.BlockSpec((tm, tn), lambda i, j: (i, j)),
        out_shape=jax.ShapeDtypeStruct((m, D_MODEL), BF16),
        compiler_params=_params("parallel", "arbitrary"),
        name="merge",
    )(h, ya, yb, yc, wt_in, wt_in, wt_in, wa, wb, wc)


def kernel(x, norm_w, w_in, gmlp_ln_w, gmlp_ln_b, gmlp_ws, gmlp_bs, conv_w, conv_b, dt_bias,
           a_log, d_skip, ssm_norm_w, w_branch_a, w_branch_b, w_branch_c, w_out, final_norm_w):
    b, s, d = x.shape
    assert (s, d) == (SEQ, D_MODEL)
    cos2, sin2 = _rope_tables()
    rows = lambda p: p[:, None, :]
    norm_w3, ln_w3, ln_b3 = rows(norm_w), rows(gmlp_ln_w), rows(gmlp_ln_b)
    conv_b3, ssm_norm_w3 = rows(conv_b), rows(ssm_norm_w)
    bs_t = jnp.swapaxes(gmlp_bs, 1, 2)
    dt_bias3, a_log3 = _pad_lanes(dt_bias), _pad_lanes(a_log)
    d_skip_e = rows(jnp.repeat(d_skip, SSM_HEAD_DIM, axis=1))
    wt_in = jnp.swapaxes(w_in, 1, 2)
    xf = x.reshape(b * s, d)
    tm, tn = PROJ_TILE_M, PROJ_TILE_N
    for i in range(DEPTH):
        h = _rmsnorm(xf, norm_w3, i, BF16)
        pa = _matmul_nt(h, wt_in, i, OFF_A, WIDTH_A, tm, tn, name="proj_a")
        pb = _matmul_nt(h, wt_in, i, OFF_B, WIDTH_B, tm, tn, name="proj_b")
        pc = _matmul_nt(h, wt_in, i, OFF_C, WIDTH_C, tm, tn, name="proj_c")
        ya = _gmlp(pa, ln_w3, ln_b3, gmlp_ws, bs_t, i)
        yb = _attention(pb, cos2, sin2)
        yc = _ssd(pc, conv_w, conv_b3, dt_bias3, a_log3, d_skip_e, ssm_norm_w3, i)
        merged = _merge(h, ya, yb, yc, wt_in, w_branch_a, w_branch_b, w_branch_c, i)
        xf = _matmul(merged, w_out, i, 0, D_MODEL, OUT_TILE_M, OUT_TILE_N, residual=xf,
                     name="out_proj")
    return _rmsnorm(xf, final_norm_w.reshape(1, 1, d), 0, x.dtype).reshape(b, s, d)
```

```python
import math

import jax
import jax.numpy as jnp
import numpy as np
from jax import lax
from jax.experimental import pallas as pl
from jax.experimental.pallas import tpu as pltpu

D_MODEL = 2048
SEQ = 2048
DEPTH = 4
BRANCH_WIDTH = 1536
GMLP_CHUNK = 128
GMLP_GROUPS = 12
HEAD_DIM = 128
HEADS_PER_GROUP = 4
DILATED_GROUPS = ((128, 1), (512, 4), (2048, 16))
ATT_BLOCK = 128
ATT_OUT_WIDTH = HEADS_PER_GROUP * HEAD_DIM
ROPE_THETA = 10000.0
SSM_D_INNER = 1536
SSM_HEAD_DIM = 64
SSM_HEADS = 24
SSM_GROUPS = 4
SSM_D_STATE = 128
SSM_CONV = 4
SSM_CHUNK = 128
SSM_CONV_DIM = SSM_D_INNER + 2 * SSM_GROUPS * SSM_D_STATE
HEAD_PAIRS = SSM_HEADS // 2
PAIRS_PER_GROUP = HEAD_PAIRS // SSM_GROUPS

OFF_A = 0
OFF_B = 3 * BRANCH_WIDTH
OFF_C = OFF_B + 3 * BRANCH_WIDTH + ATT_OUT_WIDTH
OFF_DT = OFF_C + SSM_D_INNER + SSM_CONV_DIM
OFF_GATE = OFF_DT + SSM_HEADS
WIDTH_A = OFF_B - OFF_A
WIDTH_B = OFF_C - OFF_B
WIDTH_C = 4608
DT_COL = SSM_D_INNER + SSM_CONV_DIM

LANES = 128
SUBLANES = 8
LOG2_E = 1.4426950408889634
LN_2 = 0.6931471805599453
VMEM_LIMIT = 56 * 1024 * 1024
PROJ_TILE_M, PROJ_TILE_N = 2048, 512
OUT_TILE_M = 512
MERGE_TILE_M, MERGE_TILE_N = 1024, 256

F32 = jnp.float32
BF16 = jnp.bfloat16


def _params(*semantics):
    return pltpu.CompilerParams(dimension_semantics=semantics, vmem_limit_bytes=VMEM_LIMIT)


def _layer_rows(layer, width, n_grid):
    if n_grid == 1:
        return pl.BlockSpec((None, 1, width), lambda i: (layer, 0, 0))
    return pl.BlockSpec((None, 1, width), lambda i, j: (layer, 0, 0))


def _rmsnorm_kernel(x_ref, w_ref, o_ref):
    x = x_ref[...]
    ms = jnp.mean(x * x, axis=-1, keepdims=True)
    o_ref[...] = (x * lax.rsqrt(ms + 1e-6) * w_ref[...]).astype(o_ref.dtype)


def _rmsnorm(x, w, layer, out_dtype, rows=256):
    m, d = x.shape
    return pl.pallas_call(
        _rmsnorm_kernel,
        grid=(m // rows,),
        in_specs=[pl.BlockSpec((rows, d), lambda i: (i, 0)), _layer_rows(layer, d, 1)],
        out_specs=pl.BlockSpec((rows, d), lambda i: (i, 0)),
        out_shape=jax.ShapeDtypeStruct((m, d), out_dtype),
        compiler_params=_params("parallel"),
        name="rmsnorm",
    )(x, w)


def _matmul_kernel(a_ref, w_ref, o_ref):
    o_ref[...] = jnp.dot(a_ref[...], w_ref[...].astype(BF16), preferred_element_type=F32)


def _matmul_residual_kernel(a_ref, w_ref, r_ref, o_ref):
    o_ref[...] = r_ref[...] + jnp.dot(a_ref[...], w_ref[...].astype(BF16),
                                      preferred_element_type=F32)


def _matmul(a, w, layer, col_off, n_cols, tm, tn, residual=None, name="matmul"):
    m, k = a.shape
    assert col_off % tn == 0 and n_cols % tn == 0 and m % tm == 0
    off = col_off // tn
    in_specs = [pl.BlockSpec((tm, k), lambda i, j: (i, 0)),
                pl.BlockSpec((None, k, tn), lambda i, j: (layer, 0, j + off))]
    args = [a, w]
    kern = _matmul_kernel
    if residual is not None:
        in_specs.append(pl.BlockSpec((tm, tn), lambda i, j: (i, j)))
        args.append(residual)
        kern = _matmul_residual_kernel
    return pl.pallas_call(
        kern,
        grid=(m // tm, n_cols // tn),
        in_specs=in_specs,
        out_specs=pl.BlockSpec((tm, tn), lambda i, j: (i, j)),
        out_shape=jax.ShapeDtypeStruct((m, n_cols), F32),
        compiler_params=_params("parallel", "arbitrary"),
        name=name,
    )(*args)


NT_DIMS = (((1,), (1,)), ((), ()))


def _wt_rows_spec(layer, row_off, tn, k):
    assert row_off % SUBLANES == 0 and tn % SUBLANES == 0
    return pl.BlockSpec((pl.Element(1), pl.Element(tn), pl.Element(k)),
                        lambda i, j: (layer, pl.multiple_of(row_off + j * tn, SUBLANES), 0))


def _matmul_nt_kernel(a_ref, wt_ref, o_ref):
    o_ref[...] = lax.dot_general(a_ref[...], wt_ref[0].astype(BF16), NT_DIMS,
                                 preferred_element_type=F32)


def _matmul_nt(a, wt, layer, row_off, n_cols, tm, tn, name):
    m, k = a.shape
    assert n_cols % tn == 0 and m % tm == 0
    return pl.pallas_call(
        _matmul_nt_kernel,
        grid=(m // tm, n_cols // tn),
        in_specs=[pl.BlockSpec((tm, k), lambda i, j: (i, 0)), _wt_rows_spec(layer, row_off, tn, k)],
        out_specs=pl.BlockSpec((tm, tn), lambda i, j: (i, j)),
        out_shape=jax.ShapeDtypeStruct((m, n_cols), F32),
        compiler_params=_params("parallel", "arbitrary"),
        name=name,
    )(a, wt)


GMLP_STEP_ROWS = 2 * GMLP_CHUNK

def _gmlp_kernel(u_ref, v_ref, z_ref, lnw_ref, lnb_ref, ws_ref, bst_ref, o_ref):
    row = lax.broadcasted_iota(jnp.int32, (GMLP_CHUNK, GMLP_CHUNK), 0)
    col = lax.broadcasted_iota(jnp.int32, (GMLP_CHUNK, GMLP_CHUNK), 1)
    causal = col <= row
    for c in range(GMLP_STEP_ROWS // GMLP_CHUNK):
        rows = slice(c * GMLP_CHUNK, (c + 1) * GMLP_CHUNK)
        v = jax.nn.gelu(v_ref[rows, :])
        mu = jnp.mean(v, axis=-1, keepdims=True)
        vc = v - mu
        var = jnp.mean(vc * vc, axis=-1, keepdims=True)
        vn = vc * lax.rsqrt(var + 1e-5) * lnw_ref[...] + lnb_ref[...]
        for g in range(GMLP_GROUPS):
            sl = slice(g * LANES, (g + 1) * LANES)
            wm = jnp.where(causal, ws_ref[g], 0.0).astype(BF16)
            mixed = jnp.dot(wm, vn[:, sl].astype(BF16), preferred_element_type=F32)
            mixed = mixed + bst_ref[:, g:g + 1]
            u = jax.nn.gelu(u_ref[rows, sl])
            o_ref[rows, sl] = (u * mixed * jax.nn.silu(z_ref[rows, sl])).astype(o_ref.dtype)


def _gmlp(pa, ln_w, ln_b, ws, bs_t, layer):
    m = pa.shape[0]
    w = BRANCH_WIDTH
    return pl.pallas_call(
        _gmlp_kernel,
        grid=(m // GMLP_STEP_ROWS,),
        in_specs=[pl.BlockSpec((GMLP_STEP_ROWS, w), lambda i: (i, 0)),
                  pl.BlockSpec((GMLP_STEP_ROWS, w), lambda i: (i, 1)),
                  pl.BlockSpec((GMLP_STEP_ROWS, w), lambda i: (i, 2)),
                  _layer_rows(layer, w, 1),
                  _layer_rows(layer, w, 1),
                  pl.BlockSpec((None, GMLP_GROUPS, GMLP_CHUNK, GMLP_CHUNK),
                               lambda i: (layer, 0, 0, 0)),
                  pl.BlockSpec((None, GMLP_CHUNK, GMLP_GROUPS), lambda i: (layer, 0, 0))],
        out_specs=pl.BlockSpec((GMLP_STEP_ROWS, w), lambda i: (i, 0)),
        out_shape=jax.ShapeDtypeStruct((m, w), BF16),
        compiler_params=_params("parallel"),
        name="gmlp",
    )(pa, pa, pa, ln_w, ln_b, ws, bs_t)


ROPE_ROWS = 256
ATT_UNROLL = 8


def _attn_kernel(q1, q2, q3, k1, k2, k3, v1, v2, v3, z_ref, cos_ref, sin_ref, o_ref,
                 qs1, qs2, qs3, ks1, ks2, ks3, os1, os2, os3, ls1, ls2, ls3,
                 sc1, sc2, sc3, mx1, mx2, mx3):
    scale = LOG2_E / math.sqrt(HEAD_DIM)

    def rope_step(i, carry):
        rows = pl.ds(pl.multiple_of(i * ROPE_ROWS, ROPE_ROWS), ROPE_ROWS)
        c = cos_ref[rows, :]
        s = sin_ref[rows, :]
        for src, dst, mul in ((q1, qs1, scale), (q2, qs2, scale), (q3, qs3, scale),
                              (k1, ks1, None), (k2, ks2, None), (k3, ks3, None)):
            t = src[rows, :]
            r = t * c + pltpu.roll(t, HEAD_DIM // 2, 1) * s
            dst[rows, :] = r if mul is None else r * mul
        return carry

    lax.fori_loop(0, SEQ // ROPE_ROWS, rope_step, 0)

    row = lax.broadcasted_iota(jnp.int32, (ATT_BLOCK, ATT_BLOCK), 0)
    col = lax.broadcasted_iota(jnp.int32, (ATT_BLOCK, ATT_BLOCK), 1)
    lower = col <= row
    upper = col >= row
    nt = (((1,), (1,)), ((), ()))
    ones = jnp.ones((ATT_BLOCK, HEAD_DIM), BF16)

    def with_ones(v):
        return jnp.concatenate([v.astype(BF16), ones], axis=1)

    def block_rows(dil, idx):
        def rows_at(start):
            if dil == 1:
                return pl.ds(pl.multiple_of(start, ATT_BLOCK), ATT_BLOCK)
            return pl.ds(start, ATT_BLOCK, stride=dil)

        r = idx % dil
        bi = idx // dil
        cur = rows_at(bi * (ATT_BLOCK * dil) + r)
        prev = rows_at(jnp.maximum(bi - 1, 0) * (ATT_BLOCK * dil) + r)
        return cur, prev, bi > 0

    def scores(qs, ks, sc, mx, dil, idx):
        cur, prev, has_prev = block_rows(dil, idx)
        q = qs[cur, :].astype(BF16)
        s_cur = lax.dot_general(q, ks[cur, :].astype(BF16), nt, preferred_element_type=F32)
        s_cur = jnp.where(lower, s_cur, -jnp.inf)
        sc[idx, :, 0:ATT_BLOCK] = s_cur
        if SEQ // dil > ATT_BLOCK:
            s_prev = lax.dot_general(q, ks[prev, :].astype(BF16), nt,
                                     preferred_element_type=F32)
            s_prev = jnp.where(jnp.logical_and(upper, has_prev), s_prev, -jnp.inf)
            sc[idx, :, ATT_BLOCK:2 * ATT_BLOCK] = s_prev
            s_cur = jnp.maximum(s_cur, s_prev)
        mx[idx] = jnp.broadcast_to(jnp.max(s_cur, axis=-1, keepdims=True), (ATT_BLOCK, HEAD_DIM))

    def values(v_ref, sc, mx, os, ls, dil, idx):
        cur, prev, _ = block_rows(dil, idx)
        m = mx[idx]
        acc = jnp.dot(jnp.exp2(sc[idx, :, 0:ATT_BLOCK] - m).astype(BF16),
                      with_ones(v_ref[cur, :]), preferred_element_type=F32)
        if SEQ // dil > ATT_BLOCK:
            acc = acc + jnp.dot(jnp.exp2(sc[idx, :, ATT_BLOCK:2 * ATT_BLOCK] - m).astype(BF16),
                                with_ones(v_ref[prev, :]), preferred_element_type=F32)
        l = acc[:, HEAD_DIM:]
        os[cur, :] = acc[:, :HEAD_DIM] / l
        ls[cur, :] = m * LN_2 + jnp.log(l)

    groups = ((qs1, ks1, v1, os1, ls1, sc1, mx1, DILATED_GROUPS[0][1]),
              (qs2, ks2, v2, os2, ls2, sc2, mx2, DILATED_GROUPS[1][1]),
              (qs3, ks3, v3, os3, ls3, sc3, mx3, DILATED_GROUPS[2][1]))
    n_idx = SEQ // ATT_BLOCK

    def scores_step(it, carry):
        for u in range(ATT_UNROLL):
            for qs, ks, _, _, _, sc, mx, dil in groups:
                scores(qs, ks, sc, mx, dil, it * ATT_UNROLL + u)
        return carry

    def values_step(it, carry):
        for u in range(ATT_UNROLL):
            for _, _, v_ref, os, ls, sc, mx, dil in groups:
                values(v_ref, sc, mx, os, ls, dil, it * ATT_UNROLL + u)
        return carry

    lax.fori_loop(0, n_idx // ATT_UNROLL, scores_step, 0)
    lax.fori_loop(0, n_idx // ATT_UNROLL, values_step, 0)

    def combine_step(i, carry):
        rows = pl.ds(pl.multiple_of(i * ROPE_ROWS, ROPE_ROWS), ROPE_ROWS)
        l1 = ls1[rows, :]
        l2 = ls2[rows, :]
        l3 = ls3[rows, :]
        mx = jnp.maximum(jnp.maximum(l1, l2), l3)
        e1 = jnp.exp(l1 - mx)
        e2 = jnp.exp(l2 - mx)
        e3 = jnp.exp(l3 - mx)
        o = (e1 * os1[rows, :] + e2 * os2[rows, :] + e3 * os3[rows, :]) / (e1 + e2 + e3)
        o_ref[rows, :] = (o * jax.nn.silu(z_ref[rows, :])).astype(o_ref.dtype)
        return carry

    lax.fori_loop(0, SEQ // ROPE_ROWS, combine_step, 0)


def _rope_tables():
    inv = 1.0 / (ROPE_THETA ** (jnp.arange(0, HEAD_DIM, 2, dtype=F32) / HEAD_DIM))
    ang = jnp.arange(SEQ, dtype=F32)[:, None] * inv[None, :]
    cos, sin = jnp.cos(ang), jnp.sin(ang)
    return jnp.concatenate([cos, cos], axis=-1), jnp.concatenate([-sin, sin], axis=-1)


def _attention(pb, cos2, sin2):
    m = pb.shape[0]
    batch = m // SEQ
    hpg = HEADS_PER_GROUP
    n_qkv = 3 * hpg
    n_blocks = SEQ // ATT_BLOCK

    first = OFF_B // HEAD_DIM

    def head_spec(seg, grp):
        return pl.BlockSpec((SEQ, HEAD_DIM),
                            lambda b, j: (b, first + seg * n_qkv + grp * hpg + j))

    in_specs = [head_spec(seg, grp) for seg in range(3) for grp in range(3)]
    in_specs.append(pl.BlockSpec((SEQ, HEAD_DIM), lambda b, j: (b, first + 3 * n_qkv + j)))
    in_specs += [pl.BlockSpec((SEQ, HEAD_DIM), lambda b, j: (0, 0))] * 2
    return pl.pallas_call(
        _attn_kernel,
        grid=(batch, hpg),
        in_specs=in_specs,
        out_specs=pl.BlockSpec((SEQ, HEAD_DIM), lambda b, j: (b, j)),
        out_shape=jax.ShapeDtypeStruct((m, ATT_OUT_WIDTH), BF16),
        scratch_shapes=([pltpu.VMEM((SEQ, HEAD_DIM), F32)] * 12
                        + [pltpu.VMEM((n_blocks, ATT_BLOCK, 2 * ATT_BLOCK), F32)] * 2
                        + [pltpu.VMEM((n_blocks, ATT_BLOCK, ATT_BLOCK), F32)] * 4),
        compiler_params=_params("parallel", "parallel"),
        name="dilated_attention",
    )(*([pb] * 10), cos2, sin2)


CONV_HALO = 8
CONV_TILES = SSM_CONV_DIM // LANES


def _ssd_kernel(p_ref, convw_ref, convb_ref, dtb_ref, alog_ref, dskip_ref, normw_ref,
                o_ref, xext, xcs, ys, hstate):
    c = pl.program_id(1)
    L = SSM_CHUNK

    @pl.when(c == 0)
    def _():
        hstate[...] = jnp.zeros_like(hstate)
        xext[:, 0:CONV_HALO, :] = jnp.zeros((CONV_TILES, CONV_HALO, LANES), F32)

    for t in range(CONV_TILES):
        sl = slice(t * LANES, (t + 1) * LANES)
        xext[t, CONV_HALO:CONV_HALO + L, :] = p_ref[:, SSM_D_INNER + t * LANES:
                                                    SSM_D_INNER + (t + 1) * LANES]
        acc = jnp.broadcast_to(convb_ref[:, sl], (L, LANES))
        for k in range(SSM_CONV):
            lo = CONV_HALO - (SSM_CONV - 1) + k
            acc = acc + convw_ref[k:k + 1, sl] * xext[t, lo:lo + L, :]
        xcs[:, sl] = jax.nn.silu(acc)
        xext[t, 0:CONV_HALO, :] = xext[t, L:L + CONV_HALO, :]

    lane = lax.broadcasted_iota(jnp.int32, (L, LANES), 1)
    row = lax.broadcasted_iota(jnp.int32, (L, L), 0)
    col = lax.broadcasted_iota(jnp.int32, (L, L), 1)
    causal = col <= row
    first_head = lane < SSM_HEAD_DIM

    dt = jnp.where(lane < SSM_HEADS,
                   jax.nn.softplus(p_ref[:, DT_COL:DT_COL + LANES] + dtb_ref[...]), 0.0)
    da = dt * (-jnp.exp(alog_ref[...]))
    cum = jnp.dot(causal.astype(F32), da, precision=lax.Precision.HIGHEST,
                  preferred_element_type=F32)
    cum = cum * LOG2_E
    cum_t = cum.T
    dt_t = dt.T
    dsdt_t = jnp.exp2(cum_t[:, L - 1:L] - cum_t) * dt_t
    shift_t = cum_t - jnp.log2(dt_t)

    for g in range(SSM_GROUPS):
        b_off = SSM_D_INNER + g * SSM_D_STATE
        c_off = SSM_D_INNER + SSM_GROUPS * SSM_D_STATE + g * SSM_D_STATE
        bm_t = xcs[:, b_off:b_off + SSM_D_STATE].T
        cm = xcs[:, c_off:c_off + SSM_D_STATE].astype(BF16)
        cb = jnp.where(causal, jnp.dot(cm, bm_t.astype(BF16), preferred_element_type=F32), 0.0)
        for pj in range(PAIRS_PER_GROUP):
            pair = g * PAIRS_PER_GROUP + pj
            sl = slice(pair * LANES, (pair + 1) * LANES)
            xs = xcs[:, sl]
            xs_b = xs.astype(BF16)
            y_h, st_h, cc_h = [], [], []
            for hh in range(2):
                h = 2 * pair + hh
                ccol = jnp.broadcast_to(cum[:, h:h + 1], (L, L))
                decay_dt = jnp.exp2(jnp.where(causal, ccol - shift_t[h:h + 1, :], -jnp.inf))
                y_h.append(jnp.dot((cb * decay_dt).astype(BF16), xs_b,
                                   preferred_element_type=F32))
                bw = (bm_t * dsdt_t[h:h + 1, :]).astype(BF16)
                st_h.append(jnp.dot(bw, xs_b, preferred_element_type=F32))
                cc_h.append(ccol)
            expcum = jnp.exp2(jnp.where(first_head, cc_h[0], cc_h[1]))
            h_prev = hstate[pair]
            y = jnp.where(first_head, y_h[0], y_h[1])
            y = y + jnp.dot(cm, h_prev.astype(BF16), preferred_element_type=F32) * expcum
            ys[:, sl] = y + dskip_ref[:, sl] * xs
            hstate[pair] = (h_prev * expcum[L - 1:L, :]
                            + jnp.where(first_head, st_h[0], st_h[1]))

    gw = SSM_D_INNER // SSM_GROUPS
    for g in range(SSM_GROUPS):
        sl = slice(g * gw, (g + 1) * gw)
        y = ys[:, sl] * jax.nn.silu(p_ref[:, sl])
        ms = jnp.mean(y * y, axis=-1, keepdims=True)
        o_ref[:, sl] = (y * lax.rsqrt(ms + 1e-5) * normw_ref[:, sl]).astype(o_ref.dtype)


def _pad_lanes(v):
    return jnp.pad(v, ((0, 0), (0, LANES - v.shape[1])))[:, None, :]


def _ssd(pc, conv_w, conv_b, dt_bias, a_log, d_skip_e, norm_w, layer):
    m = pc.shape[0]
    batch = m // SEQ
    nc = SEQ // SSM_CHUNK
    return pl.pallas_call(
        _ssd_kernel,
        grid=(batch, nc),
        in_specs=[pl.BlockSpec((pl.Element(SSM_CHUNK), pl.Element(WIDTH_C)),
                               lambda b, c: (pl.multiple_of((b * nc + c) * SSM_CHUNK, SSM_CHUNK),
                                             OFF_C)),
                  pl.BlockSpec((None, SSM_CONV, SSM_CONV_DIM), lambda b, c: (layer, 0, 0)),
                  _layer_rows(layer, SSM_CONV_DIM, 2),
                  _layer_rows(layer, LANES, 2),
                  _layer_rows(layer, LANES, 2),
                  _layer_rows(layer, SSM_D_INNER, 2),
                  _layer_rows(layer, SSM_D_INNER, 2)],
        out_specs=pl.BlockSpec((SSM_CHUNK, SSM_D_INNER), lambda b, c: (b * nc + c, 0)),
        out_shape=jax.ShapeDtypeStruct((m, SSM_D_INNER), BF16),
        scratch_shapes=[pltpu.VMEM((CONV_TILES, CONV_HALO + SSM_CHUNK, LANES), F32),
                        pltpu.VMEM((SSM_CHUNK, SSM_CONV_DIM), F32),
                        pltpu.VMEM((SSM_CHUNK, SSM_D_INNER), F32),
                        pltpu.VMEM((HEAD_PAIRS, SSM_D_STATE, LANES), F32)],
        compiler_params=_params("parallel", "arbitrary"),
        name="ssd",
    )(pc, conv_w, conv_b, dt_bias, a_log, d_skip_e, norm_w)


def _merge_kernel(h_ref, ya_ref, yb_ref, yc_ref, ga_ref, gb_ref, gc_ref, wa_ref, wb_ref, wc_ref,
                  o_ref):
    h = h_ref[...]

    def branch(y_ref, g_ref, w_ref):
        gate = lax.dot_general(h, g_ref[0].astype(BF16), NT_DIMS, preferred_element_type=F32)
        return jax.nn.sigmoid(gate) * jnp.dot(y_ref[...], w_ref[...].astype(BF16),
                                              preferred_element_type=F32)
    acc = branch(ya_ref, ga_ref, wa_ref) + branch(yb_ref, gb_ref, wb_ref)
    o_ref[...] = (acc + branch(yc_ref, gc_ref, wc_ref)).astype(o_ref.dtype)


def _merge(h, ya, yb, yc, wt_in, wa, wb, wc, layer):
    m = ya.shape[0]
    tm, tn = MERGE_TILE_M, MERGE_TILE_N
    y_spec = lambda w: pl.BlockSpec((tm, w), lambda i, j: (i, 0))
    g_spec = lambda k: _wt_rows_spec(layer, OFF_GATE + k * D_MODEL, tn, D_MODEL)
    w_spec = lambda w: pl.BlockSpec((None, w, tn), lambda i, j: (layer, 0, j))
    return pl.pallas_call(
        _merge_kernel,
        grid=(m // tm, D_MODEL // tn),
        in_specs=[y_spec(D_MODEL), y_spec(BRANCH_WIDTH), y_spec(ATT_OUT_WIDTH),
                  y_spec(SSM_D_INNER), g_spec(0), g_spec(1), g_spec(2),
                  w_spec(BRANCH_WIDTH), w_spec(ATT_OUT_WIDTH), w_spec(SSM_D_INNER)],
        out_specs=pl.BlockSpec((tm, tn), lambda i, j: (i, j)),
        out_shape=jax.ShapeDtypeStruct((m, D_MODEL), BF16),
        compiler_params=_params("parallel", "arbitrary"),
        name="merge",
    )(h, ya, yb, yc, wt_in, wt_in, wt_in, wa, wb, wc)


def _out_proj_norm_kernel(a_ref, w_ref, r_ref, nw_ref, *outs):
    *x_out, h_ref = outs
    x_new = r_ref[...] + jnp.dot(a_ref[...], w_ref[...], preferred_element_type=F32)
    for x_ref in x_out:
        x_ref[...] = x_new
    ms = jnp.mean(x_new * x_new, axis=-1, keepdims=True)
    h_ref[...] = (x_new * lax.rsqrt(ms + 1e-6) * nw_ref[...]).astype(h_ref.dtype)


def _out_proj_norm(a, w, residual, norm_w, layer, h_dtype):
    m, k = a.shape
    d = D_MODEL
    tm = OUT_TILE_M
    emit_x = h_dtype == BF16
    row_spec = pl.BlockSpec((tm, d), lambda i: (i, 0))
    out_specs = [row_spec] * (2 if emit_x else 1)
    out_shape = [jax.ShapeDtypeStruct((m, d), F32)] * emit_x + [jax.ShapeDtypeStruct((m, d), h_dtype)]
    outs = pl.pallas_call(
        _out_proj_norm_kernel,
        grid=(m // tm,),
        in_specs=[pl.BlockSpec((tm, k), lambda i: (i, 0)),
                  pl.BlockSpec((None, k, d), lambda i: (layer, 0, 0),
                               pipeline_mode=pl.Buffered(1)),
                  row_spec,
                  _layer_rows(layer, d, 1)],
        out_specs=out_specs,
        out_shape=out_shape,
        compiler_params=_params("parallel"),
        name="out_proj_norm",
    )(a, w, residual, norm_w)
    return (outs[0], outs[1]) if emit_x else (None, outs[0])


def kernel(x, norm_w, w_in, gmlp_ln_w, gmlp_ln_b, gmlp_ws, gmlp_bs, conv_w, conv_b, dt_bias,
           a_log, d_skip, ssm_norm_w, w_branch_a, w_branch_b, w_branch_c, w_out, final_norm_w):
    b, s, d = x.shape
    assert (s, d) == (SEQ, D_MODEL)
    cos2, sin2 = _rope_tables()
    rows = lambda p: p[:, None, :]
    norm_w3, ln_w3, ln_b3 = rows(norm_w), rows(gmlp_ln_w), rows(gmlp_ln_b)
    conv_b3, ssm_norm_w3 = rows(conv_b), rows(ssm_norm_w)
    bs_t = jnp.swapaxes(gmlp_bs, 1, 2)
    dt_bias3, a_log3 = _pad_lanes(dt_bias), _pad_lanes(a_log)
    d_skip_e = rows(jnp.repeat(d_skip, SSM_HEAD_DIM, axis=1))
    wt_in = jnp.swapaxes(w_in, 1, 2)
    next_norm_w3 = jnp.concatenate([norm_w3[1:], final_norm_w.reshape(1, 1, d)], axis=0)
    w_out_bf = w_out.astype(BF16)
    xf = x.reshape(b * s, d)
    tm, tn = PROJ_TILE_M, PROJ_TILE_N
    h = _rmsnorm(xf, norm_w3, 0, BF16)
    for i in range(DEPTH):
        last = i == DEPTH - 1
        proj = _matmul_nt(h, wt_in, i, 0, OFF_C + WIDTH_C, tm, tn, name="in_proj")
        ya = _gmlp(proj, ln_w3, ln_b3, gmlp_ws, bs_t, i)
        yb = _attention(proj, cos2, sin2)
        yc = _ssd(proj, conv_w, conv_b3, dt_bias3, a_log3, d_skip_e, ssm_norm_w3, i)
        merged = _merge(h, ya, yb, yc, wt_in, w_branch_a, w_branch_b, w_branch_c, i)
        xf, h = _out_proj_norm(merged, w_out_bf, xf, next_norm_w3, i, x.dtype if last else BF16)
    return h.reshape(b, s, d)
```

```python
import math

import jax
import jax.numpy as jnp
import numpy as np
from jax import lax
from jax.experimental import pallas as pl
from jax.experimental.pallas import tpu as pltpu

D_MODEL = 2048
SEQ = 2048
DEPTH = 4
BRANCH_WIDTH = 1536
GMLP_CHUNK = 128
GMLP_GROUPS = 12
HEAD_DIM = 128
HEADS_PER_GROUP = 4
DILATED_GROUPS = ((128, 1), (512, 4), (2048, 16))
ATT_BLOCK = 128
ATT_OUT_WIDTH = HEADS_PER_GROUP * HEAD_DIM
ROPE_THETA = 10000.0
SSM_D_INNER = 1536
SSM_HEAD_DIM = 64
SSM_HEADS = 24
SSM_GROUPS = 4
SSM_D_STATE = 128
SSM_CONV = 4
SSM_CHUNK = 128
SSM_CONV_DIM = SSM_D_INNER + 2 * SSM_GROUPS * SSM_D_STATE
HEAD_PAIRS = SSM_HEADS // 2
PAIRS_PER_GROUP = HEAD_PAIRS // SSM_GROUPS

OFF_A = 0
OFF_B = 3 * BRANCH_WIDTH
OFF_C = OFF_B + 3 * BRANCH_WIDTH + ATT_OUT_WIDTH
OFF_DT = OFF_C + SSM_D_INNER + SSM_CONV_DIM
OFF_GATE = OFF_DT + SSM_HEADS
WIDTH_A = OFF_B - OFF_A
WIDTH_B = OFF_C - OFF_B
WIDTH_C = 4608
DT_COL = SSM_D_INNER + SSM_CONV_DIM

LANES = 128
SUBLANES = 8
LOG2_E = 1.4426950408889634
LN_2 = 0.6931471805599453
VMEM_LIMIT = 56 * 1024 * 1024
PROJ_TILE_M, PROJ_TILE_N = 2048, 512
OUT_TILE_M = 512
MERGE_TILE_M, MERGE_TILE_N = 1024, 256

F32 = jnp.float32
BF16 = jnp.bfloat16


def _params(*semantics):
    return pltpu.CompilerParams(dimension_semantics=semantics, vmem_limit_bytes=VMEM_LIMIT)


def _layer_rows(layer, width, n_grid):
    if n_grid == 1:
        return pl.BlockSpec((None, 1, width), lambda i: (layer, 0, 0))
    return pl.BlockSpec((None, 1, width), lambda i, j: (layer, 0, 0))


def _rmsnorm_kernel(x_ref, w_ref, o_ref):
    x = x_ref[...]
    ms = jnp.mean(x * x, axis=-1, keepdims=True)
    o_ref[...] = (x * lax.rsqrt(ms + 1e-6) * w_ref[...]).astype(o_ref.dtype)


def _rmsnorm(x, w, layer, out_dtype, rows=256):
    m, d = x.shape
    return pl.pallas_call(
        _rmsnorm_kernel,
        grid=(m // rows,),
        in_specs=[pl.BlockSpec((rows, d), lambda i: (i, 0)), _layer_rows(layer, d, 1)],
        out_specs=pl.BlockSpec((rows, d), lambda i: (i, 0)),
        out_shape=jax.ShapeDtypeStruct((m, d), out_dtype),
        compiler_params=_params("parallel"),
        name="rmsnorm",
    )(x, w)


def _matmul_kernel(a_ref, w_ref, o_ref):
    o_ref[...] = jnp.dot(a_ref[...], w_ref[...].astype(BF16), preferred_element_type=F32)


def _matmul_residual_kernel(a_ref, w_ref, r_ref, o_ref):
    o_ref[...] = r_ref[...] + jnp.dot(a_ref[...], w_ref[...].astype(BF16),
                                      preferred_element_type=F32)


def _matmul(a, w, layer, col_off, n_cols, tm, tn, residual=None, name="matmul"):
    m, k = a.shape
    assert col_off % tn == 0 and n_cols % tn == 0 and m % tm == 0
    off = col_off // tn
    in_specs = [pl.BlockSpec((tm, k), lambda i, j: (i, 0)),
                pl.BlockSpec((None, k, tn), lambda i, j: (layer, 0, j + off))]
    args = [a, w]
    kern = _matmul_kernel
    if residual is not None:
        in_specs.append(pl.BlockSpec((tm, tn), lambda i, j: (i, j)))
        args.append(residual)
        kern = _matmul_residual_kernel
    return pl.pallas_call(
        kern,
        grid=(m // tm, n_cols // tn),
        in_specs=in_specs,
        out_specs=pl.BlockSpec((tm, tn), lambda i, j: (i, j)),
        out_shape=jax.ShapeDtypeStruct((m, n_cols), F32),
        compiler_params=_params("parallel", "arbitrary"),
        name=name,
    )(*args)


NT_DIMS = (((1,), (1,)), ((), ()))


def _wt_rows_spec(layer, row_off, tn, k):
    assert row_off % SUBLANES == 0 and tn % SUBLANES == 0
    return pl.BlockSpec((pl.Element(1), pl.Element(tn), pl.Element(k)),
                        lambda i, j: (layer, pl.multiple_of(row_off + j * tn, SUBLANES), 0))


def _matmul_nt_kernel(a_ref, wt_ref, o_ref):
    o_ref[...] = lax.dot_general(a_ref[...], wt_ref[0].astype(BF16), NT_DIMS,
                                 preferred_element_type=F32)


def _matmul_nt(a, wt, layer, row_off, n_cols, tm, tn, name):
    m, k = a.shape
    assert n_cols % tn == 0 and m % tm == 0
    return pl.pallas_call(
        _matmul_nt_kernel,
        grid=(m // tm, n_cols // tn),
        in_specs=[pl.BlockSpec((tm, k), lambda i, j: (i, 0)), _wt_rows_spec(layer, row_off, tn, k)],
        out_specs=pl.BlockSpec((tm, tn), lambda i, j: (i, j)),
        out_shape=jax.ShapeDtypeStruct((m, n_cols), F32),
        compiler_params=_params("parallel", "arbitrary"),
        name=name,
    )(a, wt)


GMLP_STEP_ROWS = 2 * GMLP_CHUNK

GELU_K1 = -2.0 * math.sqrt(2.0 / math.pi) * LOG2_E
GELU_K3 = 0.044715 * GELU_K1


def _gelu_tanh(x):
    return x / (1.0 + jnp.exp2(x * (GELU_K1 + GELU_K3 * (x * x))))


def _gmlp_kernel(u_ref, v_ref, z_ref, lnw_ref, lnb_ref, ws_ref, bst_ref, o_ref, wm_ref):
    @pl.when(pl.program_id(0) == 0)
    def _():
        row = lax.broadcasted_iota(jnp.int32, (GMLP_CHUNK, GMLP_CHUNK), 0)
        col = lax.broadcasted_iota(jnp.int32, (GMLP_CHUNK, GMLP_CHUNK), 1)
        for g in range(GMLP_GROUPS):
            wm_ref[g] = jnp.where(col <= row, ws_ref[g], 0.0).astype(BF16)

    for c in range(GMLP_STEP_ROWS // GMLP_CHUNK):
        rows = slice(c * GMLP_CHUNK, (c + 1) * GMLP_CHUNK)
        v = _gelu_tanh(v_ref[rows, :])
        mu = jnp.mean(v, axis=-1, keepdims=True)
        vc = v - mu
        var = jnp.mean(vc * vc, axis=-1, keepdims=True)
        vn = vc * lax.rsqrt(var + 1e-5) * lnw_ref[...] + lnb_ref[...]
        for g in range(GMLP_GROUPS):
            sl = slice(g * LANES, (g + 1) * LANES)
            mixed = jnp.dot(wm_ref[g], vn[:, sl].astype(BF16), preferred_element_type=F32)
            mixed = mixed + bst_ref[:, g:g + 1]
            u = _gelu_tanh(u_ref[rows, sl])
            o_ref[rows, sl] = (u * mixed * jax.nn.silu(z_ref[rows, sl])).astype(o_ref.dtype)


def _gmlp(pa, ln_w, ln_b, ws, bs_t, layer):
    m = pa.shape[0]
    w = BRANCH_WIDTH
    return pl.pallas_call(
        _gmlp_kernel,
        grid=(m // GMLP_STEP_ROWS,),
        in_specs=[pl.BlockSpec((GMLP_STEP_ROWS, w), lambda i: (i, 0)),
                  pl.BlockSpec((GMLP_STEP_ROWS, w), lambda i: (i, 1)),
                  pl.BlockSpec((GMLP_STEP_ROWS, w), lambda i: (i, 2)),
                  _layer_rows(layer, w, 1),
                  _layer_rows(layer, w, 1),
                  pl.BlockSpec((None, GMLP_GROUPS, GMLP_CHUNK, GMLP_CHUNK),
                               lambda i: (layer, 0, 0, 0)),
                  pl.BlockSpec((None, GMLP_CHUNK, GMLP_GROUPS), lambda i: (layer, 0, 0))],
        out_specs=pl.BlockSpec((GMLP_STEP_ROWS, w), lambda i: (i, 0)),
        out_shape=jax.ShapeDtypeStruct((m, w), BF16),
        scratch_shapes=[pltpu.VMEM((GMLP_GROUPS, GMLP_CHUNK, GMLP_CHUNK), BF16)],
        compiler_params=_params("arbitrary"),
        name="gmlp",
    )(pa, pa, pa, ln_w, ln_b, ws, bs_t)


ROPE_ROWS = 256
ATT_UNROLL = 16


def _attn_kernel(q1, q2, q3, k1, k2, k3, v1, v2, v3, z_ref, cos_ref, sin_ref, o_ref,
                 qs1, qs2, qs3, ks1, ks2, ks3, os1, os2, os3, ls1, ls2, ls3,
                 sc1, sc2, sc3, mx1, mx2, mx3):
    scale = LOG2_E / math.sqrt(HEAD_DIM)

    def rope_step(i, carry):
        rows = pl.ds(pl.multiple_of(i * ROPE_ROWS, ROPE_ROWS), ROPE_ROWS)
        c = cos_ref[rows, :]
        s = sin_ref[rows, :]
        for src, dst, mul in ((q1, qs1, scale), (q2, qs2, scale), (q3, qs3, scale),
                              (k1, ks1, None), (k2, ks2, None), (k3, ks3, None)):
            t = src[rows, :]
            r = t * c + pltpu.roll(t, HEAD_DIM // 2, 1) * s
            dst[rows, :] = r if mul is None else r * mul
        return carry

    lax.fori_loop(0, SEQ // ROPE_ROWS, rope_step, 0)

    row = lax.broadcasted_iota(jnp.int32, (ATT_BLOCK, ATT_BLOCK), 0)
    col = lax.broadcasted_iota(jnp.int32, (ATT_BLOCK, ATT_BLOCK), 1)
    lower = col <= row
    upper = col >= row
    nt = (((1,), (1,)), ((), ()))
    ones = jnp.ones((ATT_BLOCK, HEAD_DIM), BF16)

    def with_ones(v):
        return jnp.concatenate([v.astype(BF16), ones], axis=1)

    def block_rows(dil, idx):
        def rows_at(start):
            if dil == 1:
                return pl.ds(pl.multiple_of(start, ATT_BLOCK), ATT_BLOCK)
            return pl.ds(start, ATT_BLOCK, stride=dil)

        r = idx % dil
        bi = idx // dil
        cur = rows_at(bi * (ATT_BLOCK * dil) + r)
        prev = rows_at(jnp.maximum(bi - 1, 0) * (ATT_BLOCK * dil) + r)
        return cur, prev, bi > 0

    def scores(qs, ks, sc, mx, dil, idx):
        cur, prev, has_prev = block_rows(dil, idx)
        q = qs[cur, :].astype(BF16)
        s_cur = lax.dot_general(q, ks[cur, :].astype(BF16), nt, preferred_element_type=F32)
        s_cur = jnp.where(lower, s_cur, -jnp.inf)
        sc[idx, :, 0:ATT_BLOCK] = s_cur
        if SEQ // dil > ATT_BLOCK:
            s_prev = lax.dot_general(q, ks[prev, :].astype(BF16), nt,
                                     preferred_element_type=F32)
            s_prev = jnp.where(jnp.logical_and(upper, has_prev), s_prev, -jnp.inf)
            sc[idx, :, ATT_BLOCK:2 * ATT_BLOCK] = s_prev
            s_cur = jnp.maximum(s_cur, s_prev)
        mx[idx] = jnp.broadcast_to(jnp.max(s_cur, axis=-1, keepdims=True), (ATT_BLOCK, HEAD_DIM))

    def values(v_ref, sc, mx, os, ls, dil, idx):
        cur, prev, _ = block_rows(dil, idx)
        m = mx[idx]
        acc = jnp.dot(jnp.exp2(sc[idx, :, 0:ATT_BLOCK] - m).astype(BF16),
                      with_ones(v_ref[cur, :]), preferred_element_type=F32)
        if SEQ // dil > ATT_BLOCK:
            acc = acc + jnp.dot(jnp.exp2(sc[idx, :, ATT_BLOCK:2 * ATT_BLOCK] - m).astype(BF16),
                                with_ones(v_ref[prev, :]), preferred_element_type=F32)
        l = acc[:, HEAD_DIM:]
        os[cur, :] = acc[:, :HEAD_DIM] / l
        ls[cur, :] = m * LN_2 + jnp.log(l)

    groups = ((qs1, ks1, v1, os1, ls1, sc1, mx1, DILATED_GROUPS[0][1]),
              (qs2, ks2, v2, os2, ls2, sc2, mx2, DILATED_GROUPS[1][1]),
              (qs3, ks3, v3, os3, ls3, sc3, mx3, DILATED_GROUPS[2][1]))
    n_idx = SEQ // ATT_BLOCK

    def scores_step(it, carry):
        for u in range(ATT_UNROLL):
            for qs, ks, _, _, _, sc, mx, dil in groups:
                scores(qs, ks, sc, mx, dil, it * ATT_UNROLL + u)
        return carry

    def values_step(it, carry):
        for u in range(ATT_UNROLL):
            for _, _, v_ref, os, ls, sc, mx, dil in groups:
                values(v_ref, sc, mx, os, ls, dil, it * ATT_UNROLL + u)
        return carry

    lax.fori_loop(0, n_idx // ATT_UNROLL, scores_step, 0)
    lax.fori_loop(0, n_idx // ATT_UNROLL, values_step, 0)

    def combine_step(i, carry):
        rows = pl.ds(pl.multiple_of(i * ROPE_ROWS, ROPE_ROWS), ROPE_ROWS)
        l1 = ls1[rows, :]
        l2 = ls2[rows, :]
        l3 = ls3[rows, :]
        mx = jnp.maximum(jnp.maximum(l1, l2), l3)
        e1 = jnp.exp(l1 - mx)
        e2 = jnp.exp(l2 - mx)
        e3 = jnp.exp(l3 - mx)
        o = (e1 * os1[rows, :] + e2 * os2[rows, :] + e3 * os3[rows, :]) / (e1 + e2 + e3)
        o_ref[rows, :] = (o * jax.nn.silu(z_ref[rows, :])).astype(o_ref.dtype)
        return carry

    lax.fori_loop(0, SEQ // ROPE_ROWS, combine_step, 0)


def _rope_tables():
    inv = 1.0 / (ROPE_THETA ** (jnp.arange(0, HEAD_DIM, 2, dtype=F32) / HEAD_DIM))
    ang = jnp.arange(SEQ, dtype=F32)[:, None] * inv[None, :]
    cos, sin = jnp.cos(ang), jnp.sin(ang)
    return jnp.concatenate([cos, cos], axis=-1), jnp.concatenate([-sin, sin], axis=-1)


def _attention(pb, cos2, sin2):
    m = pb.shape[0]
    batch = m // SEQ
    hpg = HEADS_PER_GROUP
    n_qkv = 3 * hpg
    n_blocks = SEQ // ATT_BLOCK

    first = OFF_B // HEAD_DIM

    def head_spec(seg, grp):
        return pl.BlockSpec((SEQ, HEAD_DIM),
                            lambda b, j: (b, first + seg * n_qkv + grp * hpg + j))

    in_specs = [head_spec(seg, grp) for seg in range(3) for grp in range(3)]
    in_specs.append(pl.BlockSpec((SEQ, HEAD_DIM), lambda b, j: (b, first + 3 * n_qkv + j)))
    in_specs += [pl.BlockSpec((SEQ, HEAD_DIM), lambda b, j: (0, 0))] * 2
    return pl.pallas_call(
        _attn_kernel,
        grid=(batch, hpg),
        in_specs=in_specs,
        out_specs=pl.BlockSpec((SEQ, HEAD_DIM), lambda b, j: (b, j)),
        out_shape=jax.ShapeDtypeStruct((m, ATT_OUT_WIDTH), BF16),
        scratch_shapes=([pltpu.VMEM((SEQ, HEAD_DIM), F32)] * 12
                        + [pltpu.VMEM((n_blocks, ATT_BLOCK, 2 * ATT_BLOCK), F32)] * 2
                        + [pltpu.VMEM((n_blocks, ATT_BLOCK, ATT_BLOCK), F32)] * 4),
        compiler_params=_params("parallel", "parallel"),
        name="dilated_attention",
    )(*([pb] * 10), cos2, sin2)


CONV_HALO = 8
CONV_TILES = SSM_CONV_DIM // LANES


def _ssd_kernel(p_ref, convw_ref, convb_ref, dtb_ref, alog_ref, dskip_ref, normw_ref,
                o_ref, xext, xcs, ys, hstate):
    c = pl.program_id(1)
    L = SSM_CHUNK

    @pl.when(c == 0)
    def _():
        hstate[...] = jnp.zeros_like(hstate)
        xext[:, 0:CONV_HALO, :] = jnp.zeros((CONV_TILES, CONV_HALO, LANES), F32)

    for t in range(CONV_TILES):
        sl = slice(t * LANES, (t + 1) * LANES)
        xext[t, CONV_HALO:CONV_HALO + L, :] = p_ref[:, SSM_D_INNER + t * LANES:
                                                    SSM_D_INNER + (t + 1) * LANES]
        acc = jnp.broadcast_to(convb_ref[:, sl], (L, LANES))
        for k in range(SSM_CONV):
            lo = CONV_HALO - (SSM_CONV - 1) + k
            acc = acc + convw_ref[k:k + 1, sl] * xext[t, lo:lo + L, :]
        xcs[:, sl] = jax.nn.silu(acc)
        xext[t, 0:CONV_HALO, :] = xext[t, L:L + CONV_HALO, :]

    lane = lax.broadcasted_iota(jnp.int32, (L, LANES), 1)
    row = lax.broadcasted_iota(jnp.int32, (L, L), 0)
    col = lax.broadcasted_iota(jnp.int32, (L, L), 1)
    causal = col <= row
    first_head = lane < SSM_HEAD_DIM

    dt = jnp.where(lane < SSM_HEADS,
                   jax.nn.softplus(p_ref[:, DT_COL:DT_COL + LANES] + dtb_ref[...]), 0.0)
    da = dt * (-jnp.exp(alog_ref[...]))
    cum = jnp.dot(causal.astype(F32), da, precision=lax.Precision.HIGHEST,
                  preferred_element_type=F32)
    cum = cum * LOG2_E
    cum_t = cum.T
    dt_t = dt.T
    dsdt_t = jnp.exp2(cum_t[:, L - 1:L] - cum_t) * dt_t
    shift_t = cum_t - jnp.log2(dt_t)

    for g in range(SSM_GROUPS):
        b_off = SSM_D_INNER + g * SSM_D_STATE
        c_off = SSM_D_INNER + SSM_GROUPS * SSM_D_STATE + g * SSM_D_STATE
        bm_t = xcs[:, b_off:b_off + SSM_D_STATE].T
        cm = xcs[:, c_off:c_off + SSM_D_STATE].astype(BF16)
        cb = jnp.dot(cm, bm_t.astype(BF16), preferred_element_type=F32)
        for pj in range(PAIRS_PER_GROUP):
            pair = g * PAIRS_PER_GROUP + pj
            sl = slice(pair * LANES, (pair + 1) * LANES)
            xs = xcs[:, sl]
            xs_b = xs.astype(BF16)
            y_h, st_h, cc_h = [], [], []
            for hh in range(2):
                h = 2 * pair + hh
                ccol = jnp.broadcast_to(cum[:, h:h + 1], (L, L))
                decay_dt = jnp.exp2(jnp.where(causal, ccol - shift_t[h:h + 1, :], -jnp.inf))
                y_h.append(jnp.dot((cb * decay_dt).astype(BF16), xs_b,
                                   preferred_element_type=F32))
                bw = (bm_t * dsdt_t[h:h + 1, :]).astype(BF16)
                st_h.append(jnp.dot(bw, xs_b, preferred_element_type=F32))
                cc_h.append(ccol)
            expcum = jnp.exp2(jnp.where(first_head, cc_h[0], cc_h[1]))
            h_prev = hstate[pair]
            y = jnp.where(first_head, y_h[0], y_h[1])
            y = y + jnp.dot(cm, h_prev.astype(BF16), preferred_element_type=F32) * expcum
            ys[:, sl] = y + dskip_ref[:, sl] * xs
            hstate[pair] = (h_prev * expcum[L - 1:L, :]
                            + jnp.where(first_head, st_h[0], st_h[1]))

    gw = SSM_D_INNER // SSM_GROUPS
    for g in range(SSM_GROUPS):
        sl = slice(g * gw, (g + 1) * gw)
        y = ys[:, sl] * jax.nn.silu(p_ref[:, sl])
        ms = jnp.mean(y * y, axis=-1, keepdims=True)
        o_ref[:, sl] = (y * lax.rsqrt(ms + 1e-5) * normw_ref[:, sl]).astype(o_ref.dtype)


def _pad_lanes(v):
    return jnp.pad(v, ((0, 0), (0, LANES - v.shape[1])))[:, None, :]


def _ssd(pc, conv_w, conv_b, dt_bias, a_log, d_skip_e, norm_w, layer):
    m = pc.shape[0]
    batch = m // SEQ
    nc = SEQ // SSM_CHUNK
    return pl.pallas_call(
        _ssd_kernel,
        grid=(batch, nc),
        in_specs=[pl.BlockSpec((pl.Element(SSM_CHUNK), pl.Element(WIDTH_C)),
                               lambda b, c: (pl.multiple_of((b * nc + c) * SSM_CHUNK, SSM_CHUNK),
                                             OFF_C)),
                  pl.BlockSpec((None, SSM_CONV, SSM_CONV_DIM), lambda b, c: (layer, 0, 0)),
                  _layer_rows(layer, SSM_CONV_DIM, 2),
                  _layer_rows(layer, LANES, 2),
                  _layer_rows(layer, LANES, 2),
                  _layer_rows(layer, SSM_D_INNER, 2),
                  _layer_rows(layer, SSM_D_INNER, 2)],
        out_specs=pl.BlockSpec((SSM_CHUNK, SSM_D_INNER), lambda b, c: (b * nc + c, 0)),
        out_shape=jax.ShapeDtypeStruct((m, SSM_D_INNER), BF16),
        scratch_shapes=[pltpu.VMEM((CONV_TILES, CONV_HALO + SSM_CHUNK, LANES), F32),
                        pltpu.VMEM((SSM_CHUNK, SSM_CONV_DIM), F32),
                        pltpu.VMEM((SSM_CHUNK, SSM_D_INNER), F32),
                        pltpu.VMEM((HEAD_PAIRS, SSM_D_STATE, LANES), F32)],
        compiler_params=_params("parallel", "arbitrary"),
        name="ssd",
    )(pc, conv_w, conv_b, dt_bias, a_log, d_skip_e, norm_w)


def _merge_kernel(h_ref, ya_ref, yb_ref, yc_ref, ga_ref, gb_ref, gc_ref, wa_ref, wb_ref, wc_ref,
                  o_ref):
    h = h_ref[...]

    def branch(y_ref, g_ref, w_ref):
        gate = lax.dot_general(h, g_ref[0].astype(BF16), NT_DIMS, preferred_element_type=F32)
        return jax.nn.sigmoid(gate) * jnp.dot(y_ref[...], w_ref[...].astype(BF16),
                                              preferred_element_type=F32)
    acc = branch(ya_ref, ga_ref, wa_ref) + branch(yb_ref, gb_ref, wb_ref)
    o_ref[...] = (acc + branch(yc_ref, gc_ref, wc_ref)).astype(o_ref.dtype)


def _merge(h, ya, yb, yc, wt_in, wa, wb, wc, layer):
    m = ya.shape[0]
    tm, tn = MERGE_TILE_M, MERGE_TILE_N
    y_spec = lambda w: pl.BlockSpec((tm, w), lambda i, j: (i, 0))
    g_spec = lambda k: _wt_rows_spec(layer, OFF_GATE + k * D_MODEL, tn, D_MODEL)
    w_spec = lambda w: pl.BlockSpec((None, w, tn), lambda i, j: (layer, 0, j))
    return pl.pallas_call(
        _merge_kernel,
        grid=(m // tm, D_MODEL // tn),
        in_specs=[y_spec(D_MODEL), y_spec(BRANCH_WIDTH), y_spec(ATT_OUT_WIDTH),
                  y_spec(SSM_D_INNER), g_spec(0), g_spec(1), g_spec(2),
                  w_spec(BRANCH_WIDTH), w_spec(ATT_OUT_WIDTH), w_spec(SSM_D_INNER)],
        out_specs=pl.BlockSpec((tm, tn), lambda i, j: (i, j)),
        out_shape=jax.ShapeDtypeStruct((m, D_MODEL), BF16),
        compiler_params=_params("parallel", "arbitrary"),
        name="merge",
    )(h, ya, yb, yc, wt_in, wt_in, wt_in, wa, wb, wc)


def _out_proj_norm_kernel(a_ref, w_ref, r_ref, nw_ref, *outs):
    *x_out, h_ref = outs
    x_new = r_ref[...] + jnp.dot(a_ref[...], w_ref[...], preferred_element_type=F32)
    for x_ref in x_out:
        x_ref[...] = x_new
    ms = jnp.mean(x_new * x_new, axis=-1, keepdims=True)
    h_ref[...] = (x_new * lax.rsqrt(ms + 1e-6) * nw_ref[...]).astype(h_ref.dtype)


def _out_proj_norm(a, w, residual, norm_w, layer, h_dtype):
    m, k = a.shape
    d = D_MODEL
    tm = OUT_TILE_M
    emit_x = h_dtype == BF16
    row_spec = pl.BlockSpec((tm, d), lambda i: (i, 0))
    out_specs = [row_spec] * (2 if emit_x else 1)
    out_shape = [jax.ShapeDtypeStruct((m, d), F32)] * emit_x + [jax.ShapeDtypeStruct((m, d), h_dtype)]
    outs = pl.pallas_call(
        _out_proj_norm_kernel,
        grid=(m // tm,),
        in_specs=[pl.BlockSpec((tm, k), lambda i: (i, 0)),
                  pl.BlockSpec((None, k, d), lambda i: (layer, 0, 0),
                               pipeline_mode=pl.Buffered(1)),
                  row_spec,
                  _layer_rows(layer, d, 1)],
        out_specs=out_specs,
        out_shape=out_shape,
        compiler_params=_params("parallel"),
        name="out_proj_norm",
    )(a, w, residual, norm_w)
    return (outs[0], outs[1]) if emit_x else (None, outs[0])


def kernel(x, norm_w, w_in, gmlp_ln_w, gmlp_ln_b, gmlp_ws, gmlp_bs, conv_w, conv_b, dt_bias,
           a_log, d_skip, ssm_norm_w, w_branch_a, w_branch_b, w_branch_c, w_out, final_norm_w):
    b, s, d = x.shape
    assert (s, d) == (SEQ, D_MODEL)
    cos2, sin2 = _rope_tables()
    rows = lambda p: p[:, None, :]
    norm_w3, ln_w3, ln_b3 = rows(norm_w), rows(gmlp_ln_w), rows(gmlp_ln_b)
    conv_b3, ssm_norm_w3 = rows(conv_b), rows(ssm_norm_w)
    bs_t = jnp.swapaxes(gmlp_bs, 1, 2)
    dt_bias3, a_log3 = _pad_lanes(dt_bias), _pad_lanes(a_log)
    d_skip_e = rows(jnp.repeat(d_skip, SSM_HEAD_DIM, axis=1))
    wt_in = jnp.swapaxes(w_in, 1, 2)
    next_norm_w3 = jnp.concatenate([norm_w3[1:], final_norm_w.reshape(1, 1, d)], axis=0)
    w_out_bf = w_out.astype(BF16)
    xf = x.reshape(b * s, d)
    tm, tn = PROJ_TILE_M, PROJ_TILE_N
    h = _rmsnorm(xf, norm_w3, 0, BF16)
    for i in range(DEPTH):
        last = i == DEPTH - 1
        proj = _matmul_nt(h, wt_in, i, 0, OFF_C + WIDTH_C, tm, tn, name="in_proj")
        ya = _gmlp(proj, ln_w3, ln_b3, gmlp_ws, bs_t, i)
        yb = _attention(proj, cos2, sin2)
        yc = _ssd(proj, conv_w, conv_b3, dt_bias3, a_log3, d_skip_e, ssm_norm_w3, i)
        merged = _merge(h, ya, yb, yc, wt_in, w_branch_a, w_branch_b, w_branch_c, i)
        xf, h = _out_proj_norm(merged, w_out_bf, xf, next_norm_w3, i, x.dtype if last else BF16)
    return h.reshape(b, s, d)
```

```python
import math

import jax
import jax.numpy as jnp
import numpy as np
from jax import lax
from jax.experimental import pallas as pl
from jax.experimental.pallas import tpu as pltpu

D_MODEL = 2048
SEQ = 2048
DEPTH = 4
BRANCH_WIDTH = 1536
GMLP_CHUNK = 128
GMLP_GROUPS = 12
HEAD_DIM = 128
HEADS_PER_GROUP = 4
DILATED_GROUPS = ((128, 1), (512, 4), (2048, 16))
ATT_BLOCK = 128
ATT_OUT_WIDTH = HEADS_PER_GROUP * HEAD_DIM
ROPE_THETA = 10000.0
SSM_D_INNER = 1536
SSM_HEAD_DIM = 64
SSM_HEADS = 24
SSM_GROUPS = 4
SSM_D_STATE = 128
SSM_CONV = 4
SSM_CHUNK = 128
SSM_CONV_DIM = SSM_D_INNER + 2 * SSM_GROUPS * SSM_D_STATE
HEAD_PAIRS = SSM_HEADS // 2
PAIRS_PER_GROUP = HEAD_PAIRS // SSM_GROUPS

OFF_A = 0
OFF_B = 3 * BRANCH_WIDTH
OFF_C = OFF_B + 3 * BRANCH_WIDTH + ATT_OUT_WIDTH
OFF_DT = OFF_C + SSM_D_INNER + SSM_CONV_DIM
OFF_GATE = OFF_DT + SSM_HEADS
WIDTH_A = OFF_B - OFF_A
WIDTH_B = OFF_C - OFF_B
WIDTH_C = 4608
DT_COL = SSM_D_INNER + SSM_CONV_DIM

LANES = 128
SUBLANES = 8
LOG2_E = 1.4426950408889634
LN_2 = 0.6931471805599453
VMEM_LIMIT = 56 * 1024 * 1024
PROJ_TILE_M, PROJ_TILE_N = 2048, 512
OUT_TILE_M = 512
MERGE_TILE_M, MERGE_TILE_N = 1024, 256

F32 = jnp.float32
BF16 = jnp.bfloat16


def _params(*semantics):
    return pltpu.CompilerParams(dimension_semantics=semantics, vmem_limit_bytes=VMEM_LIMIT)


def _layer_rows(layer, width, n_grid):
    if n_grid == 1:
        return pl.BlockSpec((None, 1, width), lambda i: (layer, 0, 0))
    return pl.BlockSpec((None, 1, width), lambda i, j: (layer, 0, 0))


def _rmsnorm_kernel(x_ref, w_ref, o_ref):
    x = x_ref[...]
    ms = jnp.mean(x * x, axis=-1, keepdims=True)
    o_ref[...] = (x * lax.rsqrt(ms + 1e-6) * w_ref[...]).astype(o_ref.dtype)


def _rmsnorm(x, w, layer, out_dtype, rows=256):
    m, d = x.shape
    return pl.pallas_call(
        _rmsnorm_kernel,
        grid=(m // rows,),
        in_specs=[pl.BlockSpec((rows, d), lambda i: (i, 0)), _layer_rows(layer, d, 1)],
        out_specs=pl.BlockSpec((rows, d), lambda i: (i, 0)),
        out_shape=jax.ShapeDtypeStruct((m, d), out_dtype),
        compiler_params=_params("parallel"),
        name="rmsnorm",
    )(x, w)


def _matmul_kernel(a_ref, w_ref, o_ref):
    o_ref[...] = jnp.dot(a_ref[...], w_ref[...].astype(BF16), preferred_element_type=F32)


def _matmul_residual_kernel(a_ref, w_ref, r_ref, o_ref):
    o_ref[...] = r_ref[...] + jnp.dot(a_ref[...], w_ref[...].astype(BF16),
                                      preferred_element_type=F32)


def _matmul(a, w, layer, col_off, n_cols, tm, tn, residual=None, name="matmul"):
    m, k = a.shape
    assert col_off % tn == 0 and n_cols % tn == 0 and m % tm == 0
    off = col_off // tn
    in_specs = [pl.BlockSpec((tm, k), lambda i, j: (i, 0)),
                pl.BlockSpec((None, k, tn), lambda i, j: (layer, 0, j + off))]
    args = [a, w]
    kern = _matmul_kernel
    if residual is not None:
        in_specs.append(pl.BlockSpec((tm, tn), lambda i, j: (i, j)))
        args.append(residual)
        kern = _matmul_residual_kernel
    return pl.pallas_call(
        kern,
        grid=(m // tm, n_cols // tn),
        in_specs=in_specs,
        out_specs=pl.BlockSpec((tm, tn), lambda i, j: (i, j)),
        out_shape=jax.ShapeDtypeStruct((m, n_cols), F32),
        compiler_params=_params("parallel", "arbitrary"),
        name=name,
    )(*args)


NT_DIMS = (((1,), (1,)), ((), ()))


def _wt_rows_spec(layer, row_off, tn, k):
    assert row_off % SUBLANES == 0 and tn % SUBLANES == 0
    return pl.BlockSpec((pl.Element(1), pl.Element(tn), pl.Element(k)),
                        lambda i, j: (layer, pl.multiple_of(row_off + j * tn, SUBLANES), 0))


def _matmul_nt_kernel(a_ref, wt_ref, o_ref):
    o_ref[...] = lax.dot_general(a_ref[...], wt_ref[0].astype(BF16), NT_DIMS,
                                 preferred_element_type=F32)


def _matmul_nt(a, wt, layer, row_off, n_cols, tm, tn, name):
    m, k = a.shape
    assert n_cols % tn == 0 and m % tm == 0
    return pl.pallas_call(
        _matmul_nt_kernel,
        grid=(m // tm, n_cols // tn),
        in_specs=[pl.BlockSpec((tm, k), lambda i, j: (i, 0)), _wt_rows_spec(layer, row_off, tn, k)],
        out_specs=pl.BlockSpec((tm, tn), lambda i, j: (i, j)),
        out_shape=jax.ShapeDtypeStruct((m, n_cols), F32),
        compiler_params=_params("parallel", "arbitrary"),
        name=name,
    )(a, wt)


GMLP_STEP_ROWS = 4 * GMLP_CHUNK

GELU_K1 = -2.0 * math.sqrt(2.0 / math.pi) * LOG2_E
GELU_K3 = 0.044715 * GELU_K1


def _gelu_tanh(x):
    return x / (1.0 + jnp.exp2(x * (GELU_K1 + GELU_K3 * (x * x))))


def _gmlp_kernel(u_ref, v_ref, z_ref, lnw_ref, lnb_ref, ws_ref, bst_ref, o_ref, wm_ref):
    @pl.when(pl.program_id(0) == 0)
    def _():
        row = lax.broadcasted_iota(jnp.int32, (GMLP_CHUNK, GMLP_CHUNK), 0)
        col = lax.broadcasted_iota(jnp.int32, (GMLP_CHUNK, GMLP_CHUNK), 1)
        for g in range(GMLP_GROUPS):
            wm_ref[g] = jnp.where(col <= row, ws_ref[g], 0.0).astype(BF16)

    for c in range(GMLP_STEP_ROWS // GMLP_CHUNK):
        rows = slice(c * GMLP_CHUNK, (c + 1) * GMLP_CHUNK)
        v = _gelu_tanh(v_ref[rows, :])
        mu = jnp.mean(v, axis=-1, keepdims=True)
        vc = v - mu
        var = jnp.mean(vc * vc, axis=-1, keepdims=True)
        vn = vc * lax.rsqrt(var + 1e-5) * lnw_ref[...] + lnb_ref[...]
        for g in range(GMLP_GROUPS):
            sl = slice(g * LANES, (g + 1) * LANES)
            mixed = jnp.dot(wm_ref[g], vn[:, sl].astype(BF16), preferred_element_type=F32)
            mixed = mixed + bst_ref[:, g:g + 1]
            u = _gelu_tanh(u_ref[rows, sl])
            o_ref[rows, sl] = (u * mixed * jax.nn.silu(z_ref[rows, sl])).astype(o_ref.dtype)


def _gmlp(pa, ln_w, ln_b, ws, bs_t, layer):
    m = pa.shape[0]
    w = BRANCH_WIDTH
    return pl.pallas_call(
        _gmlp_kernel,
        grid=(m // GMLP_STEP_ROWS,),
        in_specs=[pl.BlockSpec((GMLP_STEP_ROWS, w), lambda i: (i, 0)),
                  pl.BlockSpec((GMLP_STEP_ROWS, w), lambda i: (i, 1)),
                  pl.BlockSpec((GMLP_STEP_ROWS, w), lambda i: (i, 2)),
                  _layer_rows(layer, w, 1),
                  _layer_rows(layer, w, 1),
                  pl.BlockSpec((None, GMLP_GROUPS, GMLP_CHUNK, GMLP_CHUNK),
                               lambda i: (layer, 0, 0, 0)),
                  pl.BlockSpec((None, GMLP_CHUNK, GMLP_GROUPS), lambda i: (layer, 0, 0))],
        out_specs=pl.BlockSpec((GMLP_STEP_ROWS, w), lambda i: (i, 0)),
        out_shape=jax.ShapeDtypeStruct((m, w), BF16),
        scratch_shapes=[pltpu.VMEM((GMLP_GROUPS, GMLP_CHUNK, GMLP_CHUNK), BF16)],
        compiler_params=_params("arbitrary"),
        name="gmlp",
    )(pa, pa, pa, ln_w, ln_b, ws, bs_t)


ROPE_ROWS = 256
ATT_UNROLL = 16


def _attn_kernel(q1, q2, q3, k1, k2, k3, v1, v2, v3, z_ref, cos_ref, sin_ref, o_ref,
                 qs1, qs2, qs3, ks1, ks2, ks3, os1, os2, os3, ls1, ls2, ls3,
                 sc1, sc2, sc3, mx1, mx2, mx3):
    scale = LOG2_E / math.sqrt(HEAD_DIM)

    def rope_step(i, carry):
        rows = pl.ds(pl.multiple_of(i * ROPE_ROWS, ROPE_ROWS), ROPE_ROWS)
        c = cos_ref[rows, :]
        s = sin_ref[rows, :]
        for src, dst, mul in ((q1, qs1, scale), (q2, qs2, scale), (q3, qs3, scale),
                              (k1, ks1, None), (k2, ks2, None), (k3, ks3, None)):
            t = src[rows, :]
            partner = pltpu.bitcast((t * s).astype(BF16), jnp.uint32)
            partner = pltpu.bitcast(pltpu.roll(partner, HEAD_DIM // 2, 1), BF16)
            r = t * c + partner.astype(F32)
            dst[rows, :] = r if mul is None else r * mul
        return carry

    lax.fori_loop(0, SEQ // ROPE_ROWS, rope_step, 0)

    row = lax.broadcasted_iota(jnp.int32, (ATT_BLOCK, ATT_BLOCK), 0)
    col = lax.broadcasted_iota(jnp.int32, (ATT_BLOCK, ATT_BLOCK), 1)
    lower = col <= row
    upper = col >= row
    nt = (((1,), (1,)), ((), ()))
    ones = jnp.ones((ATT_BLOCK, HEAD_DIM), BF16)

    def with_ones(v):
        return jnp.concatenate([v.astype(BF16), ones], axis=1)

    def block_rows(dil, idx):
        def rows_at(start):
            if dil == 1:
                return pl.ds(pl.multiple_of(start, ATT_BLOCK), ATT_BLOCK)
            return pl.ds(start, ATT_BLOCK, stride=dil)

        r = idx % dil
        bi = idx // dil
        cur = rows_at(bi * (ATT_BLOCK * dil) + r)
        prev = rows_at(jnp.maximum(bi - 1, 0) * (ATT_BLOCK * dil) + r)
        return cur, prev, bi > 0

    def scores(qs, ks, sc, mx, dil, idx):
        cur, prev, has_prev = block_rows(dil, idx)
        q = qs[cur, :].astype(BF16)
        s_cur = lax.dot_general(q, ks[cur, :].astype(BF16), nt, preferred_element_type=F32)
        s_cur = jnp.where(lower, s_cur, -jnp.inf)
        sc[idx, :, 0:ATT_BLOCK] = s_cur
        if SEQ // dil > ATT_BLOCK:
            s_prev = lax.dot_general(q, ks[prev, :].astype(BF16), nt,
                                     preferred_element_type=F32)
            s_prev = jnp.where(jnp.logical_and(upper, has_prev), s_prev, -jnp.inf)
            sc[idx, :, ATT_BLOCK:2 * ATT_BLOCK] = s_prev
            s_cur = jnp.maximum(s_cur, s_prev)
        mx[idx] = jnp.broadcast_to(jnp.max(s_cur, axis=-1, keepdims=True), (ATT_BLOCK, HEAD_DIM))

    def values(v_ref, sc, mx, os, ls, dil, idx):
        cur, prev, _ = block_rows(dil, idx)
        m = mx[idx]
        acc = jnp.dot(jnp.exp2(sc[idx, :, 0:ATT_BLOCK] - m).astype(BF16),
                      with_ones(v_ref[cur, :]), preferred_element_type=F32)
        if SEQ // dil > ATT_BLOCK:
            acc = acc + jnp.dot(jnp.exp2(sc[idx, :, ATT_BLOCK:2 * ATT_BLOCK] - m).astype(BF16),
                                with_ones(v_ref[prev, :]), preferred_element_type=F32)
        l = acc[:, HEAD_DIM:]
        os[cur, :] = acc[:, :HEAD_DIM] / l
        ls[cur, :] = m * LN_2 + jnp.log(l)

    groups = ((qs1, ks1, v1, os1, ls1, sc1, mx1, DILATED_GROUPS[0][1]),
              (qs2, ks2, v2, os2, ls2, sc2, mx2, DILATED_GROUPS[1][1]),
              (qs3, ks3, v3, os3, ls3, sc3, mx3, DILATED_GROUPS[2][1]))
    n_idx = SEQ // ATT_BLOCK

    def scores_step(it, carry):
        for u in range(ATT_UNROLL):
            for qs, ks, _, _, _, sc, mx, dil in groups:
                scores(qs, ks, sc, mx, dil, it * ATT_UNROLL + u)
        return carry

    def values_step(it, carry):
        for u in range(ATT_UNROLL):
            for _, _, v_ref, os, ls, sc, mx, dil in groups:
                values(v_ref, sc, mx, os, ls, dil, it * ATT_UNROLL + u)
        return carry

    lax.fori_loop(0, n_idx // ATT_UNROLL, scores_step, 0)
    lax.fori_loop(0, n_idx // ATT_UNROLL, values_step, 0)

    def combine_step(i, carry):
        rows = pl.ds(pl.multiple_of(i * ROPE_ROWS, ROPE_ROWS), ROPE_ROWS)
        l1 = ls1[rows, :]
        l2 = ls2[rows, :]
        l3 = ls3[rows, :]
        mx = jnp.maximum(jnp.maximum(l1, l2), l3)
        e1 = jnp.exp(l1 - mx)
        e2 = jnp.exp(l2 - mx)
        e3 = jnp.exp(l3 - mx)
        o = (e1 * os1[rows, :] + e2 * os2[rows, :] + e3 * os3[rows, :]) / (e1 + e2 + e3)
        o_ref[rows, :] = (o * jax.nn.silu(z_ref[rows, :])).astype(o_ref.dtype)
        return carry

    lax.fori_loop(0, SEQ // ROPE_ROWS, combine_step, 0)


def _rope_tables():
    inv = 1.0 / (ROPE_THETA ** (jnp.arange(0, HEAD_DIM, 2, dtype=F32) / HEAD_DIM))
    ang = jnp.arange(SEQ, dtype=F32)[:, None] * inv[None, :]
    cos, sin = jnp.cos(ang), jnp.sin(ang)
    return jnp.concatenate([cos, cos], axis=-1), jnp.concatenate([sin, -sin], axis=-1)


def _attention(pb, cos2, sin2):
    m = pb.shape[0]
    batch = m // SEQ
    hpg = HEADS_PER_GROUP
    n_qkv = 3 * hpg
    n_blocks = SEQ // ATT_BLOCK

    first = OFF_B // HEAD_DIM

    def head_spec(seg, grp):
        return pl.BlockSpec((SEQ, HEAD_DIM),
                            lambda b, j: (b, first + seg * n_qkv + grp * hpg + j))

    in_specs = [head_spec(seg, grp) for seg in range(3) for grp in range(3)]
    in_specs.append(pl.BlockSpec((SEQ, HEAD_DIM), lambda b, j: (b, first + 3 * n_qkv + j)))
    in_specs += [pl.BlockSpec((SEQ, HEAD_DIM), lambda b, j: (0, 0))] * 2
    return pl.pallas_call(
        _attn_kernel,
        grid=(batch, hpg),
        in_specs=in_specs,
        out_specs=pl.BlockSpec((SEQ, HEAD_DIM), lambda b, j: (b, j)),
        out_shape=jax.ShapeDtypeStruct((m, ATT_OUT_WIDTH), BF16),
        scratch_shapes=([pltpu.VMEM((SEQ, HEAD_DIM), F32)] * 12
                        + [pltpu.VMEM((n_blocks, ATT_BLOCK, 2 * ATT_BLOCK), F32)] * 2
                        + [pltpu.VMEM((n_blocks, ATT_BLOCK, ATT_BLOCK), F32)] * 4),
        compiler_params=_params("parallel", "parallel"),
        name="dilated_attention",
    )(*([pb] * 10), cos2, sin2)


CONV_HALO = 8
CONV_TILES = SSM_CONV_DIM // LANES


def _ssd_kernel(p_ref, convw_ref, convb_ref, dtb_ref, alog_ref, dskip_ref, normw_ref,
                o_ref, xext, xcs, ys, hstate):
    c = pl.program_id(1)
    L = SSM_CHUNK

    @pl.when(c == 0)
    def _():
        hstate[...] = jnp.zeros_like(hstate)
        xext[:, 0:CONV_HALO, :] = jnp.zeros((CONV_TILES, CONV_HALO, LANES), F32)

    for t in range(CONV_TILES):
        sl = slice(t * LANES, (t + 1) * LANES)
        xext[t, CONV_HALO:CONV_HALO + L, :] = p_ref[:, SSM_D_INNER + t * LANES:
                                                    SSM_D_INNER + (t + 1) * LANES]
        acc = jnp.broadcast_to(convb_ref[:, sl], (L, LANES))
        for k in range(SSM_CONV):
            lo = CONV_HALO - (SSM_CONV - 1) + k
            acc = acc + convw_ref[k:k + 1, sl] * xext[t, lo:lo + L, :]
        xcs[:, sl] = jax.nn.silu(acc)
        xext[t, 0:CONV_HALO, :] = xext[t, L:L + CONV_HALO, :]

    lane = lax.broadcasted_iota(jnp.int32, (L, LANES), 1)
    row = lax.broadcasted_iota(jnp.int32, (L, L), 0)
    col = lax.broadcasted_iota(jnp.int32, (L, L), 1)
    causal = col <= row
    first_head = lane < SSM_HEAD_DIM

    dt = jnp.where(lane < SSM_HEADS,
                   jax.nn.softplus(p_ref[:, DT_COL:DT_COL + LANES] + dtb_ref[...]), 0.0)
    da = dt * (-jnp.exp(alog_ref[...]))
    cum = jnp.dot(causal.astype(F32), da, precision=lax.Precision.HIGHEST,
                  preferred_element_type=F32)
    cum = cum * LOG2_E
    cum_t = cum.T
    dt_t = dt.T
    dsdt_t = jnp.exp2(cum_t[:, L - 1:L] - cum_t) * dt_t
    shift_t = cum_t - jnp.log2(dt_t)

    for g in range(SSM_GROUPS):
        b_off = SSM_D_INNER + g * SSM_D_STATE
        c_off = SSM_D_INNER + SSM_GROUPS * SSM_D_STATE + g * SSM_D_STATE
        bm_t = xcs[:, b_off:b_off + SSM_D_STATE].T
        cm = xcs[:, c_off:c_off + SSM_D_STATE].astype(BF16)
        cb = jnp.dot(cm, bm_t.astype(BF16), preferred_element_type=F32)
        for pj in range(PAIRS_PER_GROUP):
            pair = g * PAIRS_PER_GROUP + pj
            sl = slice(pair * LANES, (pair + 1) * LANES)
            xs = xcs[:, sl]
            xs_b = xs.astype(BF16)
            y_h, st_h, cc_h = [], [], []
            for hh in range(2):
                h = 2 * pair + hh
                ccol = jnp.broadcast_to(cum[:, h:h + 1], (L, L))
                decay_dt = jnp.exp2(jnp.where(causal, ccol - shift_t[h:h + 1, :], -jnp.inf))
                y_h.append(jnp.dot((cb * decay_dt).astype(BF16), xs_b,
                                   preferred_element_type=F32))
                bw = (bm_t * dsdt_t[h:h + 1, :]).astype(BF16)
                st_h.append(jnp.dot(bw, xs_b, preferred_element_type=F32))
                cc_h.append(ccol)
            expcum = jnp.exp2(jnp.where(first_head, cc_h[0], cc_h[1]))
            h_prev = hstate[pair]
            y = jnp.where(first_head, y_h[0], y_h[1])
            y = y + jnp.dot(cm, h_prev.astype(BF16), preferred_element_type=F32) * expcum
            ys[:, sl] = y + dskip_ref[:, sl] * xs
            hstate[pair] = (h_prev * expcum[L - 1:L, :]
                            + jnp.where(first_head, st_h[0], st_h[1]))

    gw = SSM_D_INNER // SSM_GROUPS
    for g in range(SSM_GROUPS):
        sl = slice(g * gw, (g + 1) * gw)
        y = ys[:, sl] * jax.nn.silu(p_ref[:, sl])
        ms = jnp.mean(y * y, axis=-1, keepdims=True)
        o_ref[:, sl] = (y * lax.rsqrt(ms + 1e-5) * normw_ref[:, sl]).astype(o_ref.dtype)


def _pad_lanes(v):
    return jnp.pad(v, ((0, 0), (0, LANES - v.shape[1])))[:, None, :]


def _ssd(pc, conv_w, conv_b, dt_bias, a_log, d_skip_e, norm_w, layer):
    m = pc.shape[0]
    batch = m // SEQ
    nc = SEQ // SSM_CHUNK
    return pl.pallas_call(
        _ssd_kernel,
        grid=(batch, nc),
        in_specs=[pl.BlockSpec((pl.Element(SSM_CHUNK), pl.Element(WIDTH_C)),
                               lambda b, c: (pl.multiple_of((b * nc + c) * SSM_CHUNK, SSM_CHUNK),
                                             OFF_C)),
                  pl.BlockSpec((None, SSM_CONV, SSM_CONV_DIM), lambda b, c: (layer, 0, 0)),
                  _layer_rows(layer, SSM_CONV_DIM, 2),
                  _layer_rows(layer, LANES, 2),
                  _layer_rows(layer, LANES, 2),
                  _layer_rows(layer, SSM_D_INNER, 2),
                  _layer_rows(layer, SSM_D_INNER, 2)],
        out_specs=pl.BlockSpec((SSM_CHUNK, SSM_D_INNER), lambda b, c: (b * nc + c, 0)),
        out_shape=jax.ShapeDtypeStruct((m, SSM_D_INNER), BF16),
        scratch_shapes=[pltpu.VMEM((CONV_TILES, CONV_HALO + SSM_CHUNK, LANES), F32),
                        pltpu.VMEM((SSM_CHUNK, SSM_CONV_DIM), F32),
                        pltpu.VMEM((SSM_CHUNK, SSM_D_INNER), F32),
                        pltpu.VMEM((HEAD_PAIRS, SSM_D_STATE, LANES), F32)],
        compiler_params=_params("parallel", "arbitrary"),
        name="ssd",
    )(pc, conv_w, conv_b, dt_bias, a_log, d_skip_e, norm_w)


def _merge_kernel(h_ref, ya_ref, yb_ref, yc_ref, ga_ref, gb_ref, gc_ref, wa_ref, wb_ref, wc_ref,
                  o_ref):
    h = h_ref[...]

    def branch(y_ref, g_ref, w_ref):
        gate = lax.dot_general(h, g_ref[0].astype(BF16), NT_DIMS, preferred_element_type=F32)
        return jax.nn.sigmoid(gate) * jnp.dot(y_ref[...], w_ref[...].astype(BF16),
                                              preferred_element_type=F32)
    acc = branch(ya_ref, ga_ref, wa_ref) + branch(yb_ref, gb_ref, wb_ref)
    o_ref[...] = (acc + branch(yc_ref, gc_ref, wc_ref)).astype(o_ref.dtype)


def _merge(h, ya, yb, yc, wt_in, wa, wb, wc, layer):
    m = ya.shape[0]
    tm, tn = MERGE_TILE_M, MERGE_TILE_N
    y_spec = lambda w: pl.BlockSpec((tm, w), lambda i, j: (i, 0))
    g_spec = lambda k: _wt_rows_spec(layer, OFF_GATE + k * D_MODEL, tn, D_MODEL)
    w_spec = lambda w: pl.BlockSpec((None, w, tn), lambda i, j: (layer, 0, j))
    return pl.pallas_call(
        _merge_kernel,
        grid=(m // tm, D_MODEL // tn),
        in_specs=[y_spec(D_MODEL), y_spec(BRANCH_WIDTH), y_spec(ATT_OUT_WIDTH),
                  y_spec(SSM_D_INNER), g_spec(0), g_spec(1), g_spec(2),
                  w_spec(BRANCH_WIDTH), w_spec(ATT_OUT_WIDTH), w_spec(SSM_D_INNER)],
        out_specs=pl.BlockSpec((tm, tn), lambda i, j: (i, j)),
        out_shape=jax.ShapeDtypeStruct((m, D_MODEL), BF16),
        compiler_params=_params("parallel", "arbitrary"),
        name="merge",
    )(h, ya, yb, yc, wt_in, wt_in, wt_in, wa, wb, wc)


def _out_proj_norm_kernel(a_ref, w_ref, r_ref, nw_ref, *outs):
    *x_out, h_ref = outs
    x_new = r_ref[...] + jnp.dot(a_ref[...], w_ref[...], preferred_element_type=F32)
    for x_ref in x_out:
        x_ref[...] = x_new
    ms = jnp.mean(x_new * x_new, axis=-1, keepdims=True)
    h_ref[...] = (x_new * lax.rsqrt(ms + 1e-6) * nw_ref[...]).astype(h_ref.dtype)


def _out_proj_norm(a, w, residual, norm_w, layer, h_dtype):
    m, k = a.shape
    d = D_MODEL
    tm = OUT_TILE_M
    emit_x = h_dtype == BF16
    row_spec = pl.BlockSpec((tm, d), lambda i: (i, 0))
    out_specs = [row_spec] * (2 if emit_x else 1)
    out_shape = [jax.ShapeDtypeStruct((m, d), F32)] * emit_x + [jax.ShapeDtypeStruct((m, d), h_dtype)]
    outs = pl.pallas_call(
        _out_proj_norm_kernel,
        grid=(m // tm,),
        in_specs=[pl.BlockSpec((tm, k), lambda i: (i, 0)),
                  pl.BlockSpec((None, k, d), lambda i: (layer, 0, 0),
                               pipeline_mode=pl.Buffered(1)),
                  row_spec,
                  _layer_rows(layer, d, 1)],
        out_specs=out_specs,
        out_shape=out_shape,
        compiler_params=_params("parallel"),
        name="out_proj_norm",
    )(a, w, residual, norm_w)
    return (outs[0], outs[1]) if emit_x else (None, outs[0])


def kernel(x, norm_w, w_in, gmlp_ln_w, gmlp_ln_b, gmlp_ws, gmlp_bs, conv_w, conv_b, dt_bias,
           a_log, d_skip, ssm_norm_w, w_branch_a, w_branch_b, w_branch_c, w_out, final_norm_w):
    b, s, d = x.shape
    assert (s, d) == (SEQ, D_MODEL)
    cos2, sin2 = _rope_tables()
    rows = lambda p: p[:, None, :]
    norm_w3, ln_w3, ln_b3 = rows(norm_w), rows(gmlp_ln_w), rows(gmlp_ln_b)
    conv_b3, ssm_norm_w3 = rows(conv_b), rows(ssm_norm_w)
    bs_t = jnp.swapaxes(gmlp_bs, 1, 2)
    dt_bias3, a_log3 = _pad_lanes(dt_bias), _pad_lanes(a_log)
    d_skip_e = rows(jnp.repeat(d_skip, SSM_HEAD_DIM, axis=1))
    wt_in = jnp.swapaxes(w_in, 1, 2)
    next_norm_w3 = jnp.concatenate([norm_w3[1:], final_norm_w.reshape(1, 1, d)], axis=0)
    w_out_bf = w_out.astype(BF16)
    xf = x.reshape(b * s, d)
    tm, tn = PROJ_TILE_M, PROJ_TILE_N
    h = _rmsnorm(xf, norm_w3, 0, BF16)
    for i in range(DEPTH):
        last = i == DEPTH - 1
        proj = _matmul_nt(h, wt_in, i, 0, OFF_C + WIDTH_C, tm, tn, name="in_proj")
        ya = _gmlp(proj, ln_w3, ln_b3, gmlp_ws, bs_t, i)
        yb = _attention(proj, cos2, sin2)
        yc = _ssd(proj, conv_w, conv_b3, dt_bias3, a_log3, d_skip_e, ssm_norm_w3, i)
        merged = _merge(h, ya, yb, yc, wt_in, w_branch_a, w_branch_b, w_branch_c, i)
        xf, h = _out_proj_norm(merged, w_out_bf, xf, next_norm_w3, i, x.dtype if last else BF16)
    return h.reshape(b, s, d)
```

```python
import math

import jax
import jax.numpy as jnp
import numpy as np
from jax import lax
from jax.experimental import pallas as pl
from jax.experimental.pallas import tpu as pltpu

D_MODEL = 2048
SEQ = 2048
DEPTH = 4
BRANCH_WIDTH = 1536
GMLP_CHUNK = 128
GMLP_GROUPS = 12
HEAD_DIM = 128
HEADS_PER_GROUP = 4
DILATED_GROUPS = ((128, 1), (512, 4), (2048, 16))
ATT_BLOCK = 128
ATT_OUT_WIDTH = HEADS_PER_GROUP * HEAD_DIM
ROPE_THETA = 10000.0
SSM_D_INNER = 1536
SSM_HEAD_DIM = 64
SSM_HEADS = 24
SSM_GROUPS = 4
SSM_D_STATE = 128
SSM_CONV = 4
SSM_CHUNK = 128
SSM_CONV_DIM = SSM_D_INNER + 2 * SSM_GROUPS * SSM_D_STATE
HEAD_PAIRS = SSM_HEADS // 2
PAIRS_PER_GROUP = HEAD_PAIRS // SSM_GROUPS

OFF_A = 0
OFF_B = 3 * BRANCH_WIDTH
OFF_C = OFF_B + 3 * BRANCH_WIDTH + ATT_OUT_WIDTH
OFF_DT = OFF_C + SSM_D_INNER + SSM_CONV_DIM
OFF_GATE = OFF_DT + SSM_HEADS
WIDTH_A = OFF_B - OFF_A
WIDTH_B = OFF_C - OFF_B
WIDTH_C = 4608
DT_COL = SSM_D_INNER + SSM_CONV_DIM

LANES = 128
SUBLANES = 8
LOG2_E = 1.4426950408889634
LN_2 = 0.6931471805599453
VMEM_LIMIT = 56 * 1024 * 1024
PROJ_TILE_M, PROJ_TILE_N = 2048, 512
OUT_TILE_M = 512
MERGE_TILE_M, MERGE_TILE_N = 1024, 256

F32 = jnp.float32
BF16 = jnp.bfloat16


def _params(*semantics):
    return pltpu.CompilerParams(dimension_semantics=semantics, vmem_limit_bytes=VMEM_LIMIT)


def _layer_rows(layer, width, n_grid):
    if n_grid == 1:
        return pl.BlockSpec((None, 1, width), lambda i: (layer, 0, 0))
    return pl.BlockSpec((None, 1, width), lambda i, j: (layer, 0, 0))


def _rmsnorm_kernel(x_ref, w_ref, o_ref):
    x = x_ref[...]
    ms = jnp.mean(x * x, axis=-1, keepdims=True)
    o_ref[...] = (x * lax.rsqrt(ms + 1e-6) * w_ref[...]).astype(o_ref.dtype)


def _rmsnorm(x, w, layer, out_dtype, rows=256):
    m, d = x.shape
    return pl.pallas_call(
        _rmsnorm_kernel,
        grid=(m // rows,),
        in_specs=[pl.BlockSpec((rows, d), lambda i: (i, 0)), _layer_rows(layer, d, 1)],
        out_specs=pl.BlockSpec((rows, d), lambda i: (i, 0)),
        out_shape=jax.ShapeDtypeStruct((m, d), out_dtype),
        compiler_params=_params("parallel"),
        name="rmsnorm",
    )(x, w)


def _matmul_kernel(a_ref, w_ref, o_ref):
    o_ref[...] = jnp.dot(a_ref[...], w_ref[...].astype(BF16), preferred_element_type=F32)


def _matmul_residual_kernel(a_ref, w_ref, r_ref, o_ref):
    o_ref[...] = r_ref[...] + jnp.dot(a_ref[...], w_ref[...].astype(BF16),
                                      preferred_element_type=F32)


def _matmul(a, w, layer, col_off, n_cols, tm, tn, residual=None, name="matmul"):
    m, k = a.shape
    assert col_off % tn == 0 and n_cols % tn == 0 and m % tm == 0
    off = col_off // tn
    in_specs = [pl.BlockSpec((tm, k), lambda i, j: (i, 0)),
                pl.BlockSpec((None, k, tn), lambda i, j: (layer, 0, j + off))]
    args = [a, w]
    kern = _matmul_kernel
    if residual is not None:
        in_specs.append(pl.BlockSpec((tm, tn), lambda i, j: (i, j)))
        args.append(residual)
        kern = _matmul_residual_kernel
    return pl.pallas_call(
        kern,
        grid=(m // tm, n_cols // tn),
        in_specs=in_specs,
        out_specs=pl.BlockSpec((tm, tn), lambda i, j: (i, j)),
        out_shape=jax.ShapeDtypeStruct((m, n_cols), F32),
        compiler_params=_params("parallel", "arbitrary"),
        name=name,
    )(*args)


NT_DIMS = (((1,), (1,)), ((), ()))


def _wt_rows_spec(layer, row_off, tn, k):
    assert row_off % SUBLANES == 0 and tn % SUBLANES == 0
    return pl.BlockSpec((pl.Element(1), pl.Element(tn), pl.Element(k)),
                        lambda i, j: (layer, pl.multiple_of(row_off + j * tn, SUBLANES), 0))


def _matmul_nt_kernel(a_ref, wt_ref, o_ref):
    o_ref[...] = lax.dot_general(a_ref[...], wt_ref[0].astype(BF16), NT_DIMS,
                                 preferred_element_type=F32)


def _matmul_nt(a, wt, layer, row_off, n_cols, tm, tn, name):
    m, k = a.shape
    assert n_cols % tn == 0 and m % tm == 0
    return pl.pallas_call(
        _matmul_nt_kernel,
        grid=(m // tm, n_cols // tn),
        in_specs=[pl.BlockSpec((tm, k), lambda i, j: (i, 0)), _wt_rows_spec(layer, row_off, tn, k)],
        out_specs=pl.BlockSpec((tm, tn), lambda i, j: (i, j)),
        out_shape=jax.ShapeDtypeStruct((m, n_cols), F32),
        compiler_params=_params("parallel", "arbitrary"),
        name=name,
    )(a, wt)


GMLP_STEP_ROWS = 4 * GMLP_CHUNK

GELU_K1 = -2.0 * math.sqrt(2.0 / math.pi) * LOG2_E
GELU_K3 = 0.044715 * GELU_K1


def _gelu_tanh(x):
    return x / (1.0 + jnp.exp2(x * (GELU_K1 + GELU_K3 * (x * x))))


def _gmlp_kernel(u_ref, v_ref, z_ref, lnw_ref, lnb_ref, ws_ref, bst_ref, o_ref, wm_ref):
    @pl.when(pl.program_id(0) == 0)
    def _():
        row = lax.broadcasted_iota(jnp.int32, (GMLP_CHUNK, GMLP_CHUNK), 0)
        col = lax.broadcasted_iota(jnp.int32, (GMLP_CHUNK, GMLP_CHUNK), 1)
        for g in range(GMLP_GROUPS):
            wm_ref[g] = jnp.where(col <= row, ws_ref[g], 0.0).astype(BF16)

    for c in range(GMLP_STEP_ROWS // GMLP_CHUNK):
        rows = slice(c * GMLP_CHUNK, (c + 1) * GMLP_CHUNK)
        v = _gelu_tanh(v_ref[rows, :])
        mu = jnp.mean(v, axis=-1, keepdims=True)
        vc = v - mu
        var = jnp.mean(vc * vc, axis=-1, keepdims=True)
        vn = vc * lax.rsqrt(var + 1e-5) * lnw_ref[...] + lnb_ref[...]
        for g in range(GMLP_GROUPS):
            sl = slice(g * LANES, (g + 1) * LANES)
            mixed = jnp.dot(wm_ref[g], vn[:, sl].astype(BF16), preferred_element_type=F32)
            mixed = mixed + bst_ref[:, g:g + 1]
            u = _gelu_tanh(u_ref[rows, sl])
            o_ref[rows, sl] = (u * mixed * jax.nn.silu(z_ref[rows, sl])).astype(o_ref.dtype)


def _gmlp(pa, ln_w, ln_b, ws, bs_t, layer):
    m = pa.shape[0]
    w = BRANCH_WIDTH
    return pl.pallas_call(
        _gmlp_kernel,
        grid=(m // GMLP_STEP_ROWS,),
        in_specs=[pl.BlockSpec((GMLP_STEP_ROWS, w), lambda i: (i, 0)),
                  pl.BlockSpec((GMLP_STEP_ROWS, w), lambda i: (i, 1)),
                  pl.BlockSpec((GMLP_STEP_ROWS, w), lambda i: (i, 2)),
                  _layer_rows(layer, w, 1),
                  _layer_rows(layer, w, 1),
                  pl.BlockSpec((None, GMLP_GROUPS, GMLP_CHUNK, GMLP_CHUNK),
                               lambda i: (layer, 0, 0, 0)),
                  pl.BlockSpec((None, GMLP_CHUNK, GMLP_GROUPS), lambda i: (layer, 0, 0))],
        out_specs=pl.BlockSpec((GMLP_STEP_ROWS, w), lambda i: (i, 0)),
        out_shape=jax.ShapeDtypeStruct((m, w), BF16),
        scratch_shapes=[pltpu.VMEM((GMLP_GROUPS, GMLP_CHUNK, GMLP_CHUNK), BF16)],
        compiler_params=_params("arbitrary"),
        name="gmlp",
    )(pa, pa, pa, ln_w, ln_b, ws, bs_t)


ROPE_ROWS = 256
ATT_UNROLL = 16


def _attn_kernel(q1, q2, q3, k1, k2, k3, v1, v2, v3, z_ref, cos_ref, sin_ref, o_ref,
                 qs1, qs2, qs3, ks1, ks2, ks3, os1, os2, os3, ls1, ls2, ls3,
                 sc1, sc2, sc3, mx1, mx2, mx3):
    scale = LOG2_E / math.sqrt(HEAD_DIM)

    def rope_step(i, carry):
        rows = pl.ds(pl.multiple_of(i * ROPE_ROWS, ROPE_ROWS), ROPE_ROWS)
        c = cos_ref[rows, :]
        s = sin_ref[rows, :]
        for src, dst, mul in ((q1, qs1, scale), (q2, qs2, scale), (q3, qs3, scale),
                              (k1, ks1, None), (k2, ks2, None), (k3, ks3, None)):
            t = src[rows, :]
            partner = pltpu.bitcast((t * s).astype(BF16), jnp.uint32)
            partner = pltpu.bitcast(pltpu.roll(partner, HEAD_DIM // 2, 1), BF16)
            r = t * c + partner.astype(F32)
            dst[rows, :] = r if mul is None else r * mul
        return carry

    for i in range(SEQ // ROPE_ROWS):
        rope_step(i, 0)

    row = lax.broadcasted_iota(jnp.int32, (ATT_BLOCK, ATT_BLOCK), 0)
    col = lax.broadcasted_iota(jnp.int32, (ATT_BLOCK, ATT_BLOCK), 1)
    lower = col <= row
    upper = col >= row
    nt = (((1,), (1,)), ((), ()))
    ones = jnp.ones((ATT_BLOCK, HEAD_DIM), BF16)

    def with_ones(v):
        return jnp.concatenate([v.astype(BF16), ones], axis=1)

    def block_rows(dil, idx):
        def rows_at(start):
            if dil == 1:
                return pl.ds(pl.multiple_of(start, ATT_BLOCK), ATT_BLOCK)
            return pl.ds(start, ATT_BLOCK, stride=dil)

        r = idx % dil
        bi = idx // dil
        cur = rows_at(bi * (ATT_BLOCK * dil) + r)
        prev = rows_at(jnp.maximum(bi - 1, 0) * (ATT_BLOCK * dil) + r)
        return cur, prev, bi > 0

    def scores(qs, ks, sc, mx, dil, idx):
        cur, prev, has_prev = block_rows(dil, idx)
        q = qs[cur, :].astype(BF16)
        s_cur = lax.dot_general(q, ks[cur, :].astype(BF16), nt, preferred_element_type=F32)
        s_cur = jnp.where(lower, s_cur, -jnp.inf)
        sc[idx, :, 0:ATT_BLOCK] = s_cur
        if SEQ // dil > ATT_BLOCK:
            s_prev = lax.dot_general(q, ks[prev, :].astype(BF16), nt,
                                     preferred_element_type=F32)
            s_prev = jnp.where(jnp.logical_and(upper, has_prev), s_prev, -jnp.inf)
            sc[idx, :, ATT_BLOCK:2 * ATT_BLOCK] = s_prev
            s_cur = jnp.maximum(s_cur, s_prev)
        mx[idx] = jnp.broadcast_to(jnp.max(s_cur, axis=-1, keepdims=True), (ATT_BLOCK, HEAD_DIM))

    def values(v_ref, sc, mx, os, ls, dil, idx):
        cur, prev, _ = block_rows(dil, idx)
        m = mx[idx]
        acc = jnp.dot(jnp.exp2(sc[idx, :, 0:ATT_BLOCK] - m).astype(BF16),
                      with_ones(v_ref[cur, :]), preferred_element_type=F32)
        if SEQ // dil > ATT_BLOCK:
            acc = acc + jnp.dot(jnp.exp2(sc[idx, :, ATT_BLOCK:2 * ATT_BLOCK] - m).astype(BF16),
                                with_ones(v_ref[prev, :]), preferred_element_type=F32)
        l = acc[:, HEAD_DIM:]
        os[cur, :] = acc[:, :HEAD_DIM] / l
        ls[cur, :] = m * LN_2 + jnp.log(l)

    groups = ((qs1, ks1, v1, os1, ls1, sc1, mx1, DILATED_GROUPS[0][1]),
              (qs2, ks2, v2, os2, ls2, sc2, mx2, DILATED_GROUPS[1][1]),
              (qs3, ks3, v3, os3, ls3, sc3, mx3, DILATED_GROUPS[2][1]))
    n_idx = SEQ // ATT_BLOCK

    def scores_step(it, carry):
        for u in range(ATT_UNROLL):
            for qs, ks, _, _, _, sc, mx, dil in groups:
                scores(qs, ks, sc, mx, dil, it * ATT_UNROLL + u)
        return carry

    def values_step(it, carry):
        for u in range(ATT_UNROLL):
            for _, _, v_ref, os, ls, sc, mx, dil in groups:
                values(v_ref, sc, mx, os, ls, dil, it * ATT_UNROLL + u)
        return carry

    lax.fori_loop(0, n_idx // ATT_UNROLL, scores_step, 0)
    lax.fori_loop(0, n_idx // ATT_UNROLL, values_step, 0)

    def combine_step(i, carry):
        rows = pl.ds(pl.multiple_of(i * ROPE_ROWS, ROPE_ROWS), ROPE_ROWS)
        l1 = ls1[rows, :]
        l2 = ls2[rows, :]
        l3 = ls3[rows, :]
        mx = jnp.maximum(jnp.maximum(l1, l2), l3)
        e1 = jnp.exp(l1 - mx)
        e2 = jnp.exp(l2 - mx)
        e3 = jnp.exp(l3 - mx)
        o = (e1 * os1[rows, :] + e2 * os2[rows, :] + e3 * os3[rows, :]) / (e1 + e2 + e3)
        o_ref[rows, :] = (o * jax.nn.silu(z_ref[rows, :])).astype(o_ref.dtype)
        return carry

    lax.fori_loop(0, SEQ // ROPE_ROWS, combine_step, 0)


def _rope_tables():
    inv = 1.0 / (ROPE_THETA ** (jnp.arange(0, HEAD_DIM, 2, dtype=F32) / HEAD_DIM))
    ang = jnp.arange(SEQ, dtype=F32)[:, None] * inv[None, :]
    cos, sin = jnp.cos(ang), jnp.sin(ang)
    return jnp.concatenate([cos, cos], axis=-1), jnp.concatenate([sin, -sin], axis=-1)


def _attention(pb, cos2, sin2):
    m = pb.shape[0]
    batch = m // SEQ
    hpg = HEADS_PER_GROUP
    n_qkv = 3 * hpg
    n_blocks = SEQ // ATT_BLOCK

    first = OFF_B // HEAD_DIM

    def head_spec(seg, grp):
        return pl.BlockSpec((SEQ, HEAD_DIM),
                            lambda b, j: (b, first + seg * n_qkv + grp * hpg + j))

    in_specs = [head_spec(seg, grp) for seg in range(3) for grp in range(3)]
    in_specs.append(pl.BlockSpec((SEQ, HEAD_DIM), lambda b, j: (b, first + 3 * n_qkv + j)))
    in_specs += [pl.BlockSpec((SEQ, HEAD_DIM), lambda b, j: (0, 0))] * 2
    return pl.pallas_call(
        _attn_kernel,
        grid=(batch, hpg),
        in_specs=in_specs,
        out_specs=pl.BlockSpec((SEQ, HEAD_DIM), lambda b, j: (b, j)),
        out_shape=jax.ShapeDtypeStruct((m, ATT_OUT_WIDTH), BF16),
        scratch_shapes=([pltpu.VMEM((SEQ, HEAD_DIM), F32)] * 12
                        + [pltpu.VMEM((n_blocks, ATT_BLOCK, 2 * ATT_BLOCK), F32)] * 2
                        + [pltpu.VMEM((n_blocks, ATT_BLOCK, ATT_BLOCK), F32)] * 4),
        compiler_params=_params("parallel", "parallel"),
        name="dilated_attention",
    )(*([pb] * 10), cos2, sin2)


CONV_HALO = 8
CONV_TILES = SSM_CONV_DIM // LANES


def _ssd_kernel(p_ref, convw_ref, convb_ref, dtb_ref, alog_ref, dskip_ref, normw_ref,
                o_ref, xext, xcs, ys, hstate):
    c = pl.program_id(1)
    L = SSM_CHUNK

    @pl.when(c == 0)
    def _():
        hstate[...] = jnp.zeros_like(hstate)
        xext[:, 0:CONV_HALO, :] = jnp.zeros((CONV_TILES, CONV_HALO, LANES), F32)

    for t in range(CONV_TILES):
        sl = slice(t * LANES, (t + 1) * LANES)
        xext[t, CONV_HALO:CONV_HALO + L, :] = p_ref[:, SSM_D_INNER + t * LANES:
                                                    SSM_D_INNER + (t + 1) * LANES]
        acc = jnp.broadcast_to(convb_ref[:, sl], (L, LANES))
        for k in range(SSM_CONV):
            lo = CONV_HALO - (SSM_CONV - 1) + k
            acc = acc + convw_ref[k:k + 1, sl] * xext[t, lo:lo + L, :]
        xcs[:, sl] = jax.nn.silu(acc)
        xext[t, 0:CONV_HALO, :] = xext[t, L:L + CONV_HALO, :]

    lane = lax.broadcasted_iota(jnp.int32, (L, LANES), 1)
    row = lax.broadcasted_iota(jnp.int32, (L, L), 0)
    col = lax.broadcasted_iota(jnp.int32, (L, L), 1)
    causal = col <= row
    first_head = lane < SSM_HEAD_DIM

    dt = jnp.where(lane < SSM_HEADS,
                   jax.nn.softplus(p_ref[:, DT_COL:DT_COL + LANES] + dtb_ref[...]), 0.0)
    da = dt * (-jnp.exp(alog_ref[...]))
    cum = jnp.dot(causal.astype(F32), da, precision=lax.Precision.HIGHEST,
                  preferred_element_type=F32)
    cum = cum * LOG2_E
    cum_t = cum.T
    dt_t = dt.T
    dsdt_t = jnp.exp2(cum_t[:, L - 1:L] - cum_t) * dt_t
    shift_t = cum_t - jnp.log2(dt_t)

    for g in range(SSM_GROUPS):
        b_off = SSM_D_INNER + g * SSM_D_STATE
        c_off = SSM_D_INNER + SSM_GROUPS * SSM_D_STATE + g * SSM_D_STATE
        bm_t = xcs[:, b_off:b_off + SSM_D_STATE].T
        cm = xcs[:, c_off:c_off + SSM_D_STATE].astype(BF16)
        cb = jnp.dot(cm, bm_t.astype(BF16), preferred_element_type=F32)
        for pj in range(PAIRS_PER_GROUP):
            pair = g * PAIRS_PER_GROUP + pj
            sl = slice(pair * LANES, (pair + 1) * LANES)
            xs = xcs[:, sl]
            xs_b = xs.astype(BF16)
            y_h, st_h, cc_h = [], [], []
            for hh in range(2):
                h = 2 * pair + hh
                ccol = jnp.broadcast_to(cum[:, h:h + 1], (L, L))
                decay_dt = jnp.exp2(jnp.where(causal, ccol - shift_t[h:h + 1, :], -jnp.inf))
                y_h.append(jnp.dot((cb * decay_dt).astype(BF16), xs_b,
                                   preferred_element_type=F32))
                bw = (bm_t * dsdt_t[h:h + 1, :]).astype(BF16)
                st_h.append(jnp.dot(bw, xs_b, preferred_element_type=F32))
                cc_h.append(ccol)
            expcum = jnp.exp2(jnp.where(first_head, cc_h[0], cc_h[1]))
            h_prev = hstate[pair]
            y = jnp.where(first_head, y_h[0], y_h[1])
            y = y + jnp.dot(cm, h_prev.astype(BF16), preferred_element_type=F32) * expcum
            ys[:, sl] = y + dskip_ref[:, sl] * xs
            hstate[pair] = (h_prev * expcum[L - 1:L, :]
                            + jnp.where(first_head, st_h[0], st_h[1]))

    gw = SSM_D_INNER // SSM_GROUPS
    for g in range(SSM_GROUPS):
        sl = slice(g * gw, (g + 1) * gw)
        y = ys[:, sl] * jax.nn.silu(p_ref[:, sl])
        ms = jnp.mean(y * y, axis=-1, keepdims=True)
        o_ref[:, sl] = (y * lax.rsqrt(ms + 1e-5) * normw_ref[:, sl]).astype(o_ref.dtype)


def _pad_lanes(v):
    return jnp.pad(v, ((0, 0), (0, LANES - v.shape[1])))[:, None, :]


def _ssd(pc, conv_w, conv_b, dt_bias, a_log, d_skip_e, norm_w, layer):
    m = pc.shape[0]
    batch = m // SEQ
    nc = SEQ // SSM_CHUNK
    return pl.pallas_call(
        _ssd_kernel,
        grid=(batch, nc),
        in_specs=[pl.BlockSpec((pl.Element(SSM_CHUNK), pl.Element(WIDTH_C)),
                               lambda b, c: (pl.multiple_of((b * nc + c) * SSM_CHUNK, SSM_CHUNK),
                                             OFF_C)),
                  pl.BlockSpec((None, SSM_CONV, SSM_CONV_DIM), lambda b, c: (layer, 0, 0)),
                  _layer_rows(layer, SSM_CONV_DIM, 2),
                  _layer_rows(layer, LANES, 2),
                  _layer_rows(layer, LANES, 2),
                  _layer_rows(layer, SSM_D_INNER, 2),
                  _layer_rows(layer, SSM_D_INNER, 2)],
        out_specs=pl.BlockSpec((SSM_CHUNK, SSM_D_INNER), lambda b, c: (b * nc + c, 0)),
        out_shape=jax.ShapeDtypeStruct((m, SSM_D_INNER), BF16),
        scratch_shapes=[pltpu.VMEM((CONV_TILES, CONV_HALO + SSM_CHUNK, LANES), F32),
                        pltpu.VMEM((SSM_CHUNK, SSM_CONV_DIM), F32),
                        pltpu.VMEM((SSM_CHUNK, SSM_D_INNER), F32),
                        pltpu.VMEM((HEAD_PAIRS, SSM_D_STATE, LANES), F32)],
        compiler_params=_params("parallel", "arbitrary"),
        name="ssd",
    )(pc, conv_w, conv_b, dt_bias, a_log, d_skip_e, norm_w)


def _merge_kernel(h_ref, ya_ref, yb_ref, yc_ref, ga_ref, gb_ref, gc_ref, wa_ref, wb_ref, wc_ref,
                  o_ref):
    h = h_ref[...]

    def branch(y_ref, g_ref, w_ref):
        gate = lax.dot_general(h, g_ref[0].astype(BF16), NT_DIMS, preferred_element_type=F32)
        return jax.nn.sigmoid(gate) * jnp.dot(y_ref[...], w_ref[...].astype(BF16),
                                              preferred_element_type=F32)
    acc = branch(ya_ref, ga_ref, wa_ref) + branch(yb_ref, gb_ref, wb_ref)
    o_ref[...] = (acc + branch(yc_ref, gc_ref, wc_ref)).astype(o_ref.dtype)


def _merge(h, ya, yb, yc, wt_in, wa, wb, wc, layer):
    m = ya.shape[0]
    tm, tn = MERGE_TILE_M, MERGE_TILE_N
    y_spec = lambda w: pl.BlockSpec((tm, w), lambda i, j: (i, 0))
    g_spec = lambda k: _wt_rows_spec(layer, OFF_GATE + k * D_MODEL, tn, D_MODEL)
    w_spec = lambda w: pl.BlockSpec((None, w, tn), lambda i, j: (layer, 0, j))
    return pl.pallas_call(
        _merge_kernel,
        grid=(m // tm, D_MODEL // tn),
        in_specs=[y_spec(D_MODEL), y_spec(BRANCH_WIDTH), y_spec(ATT_OUT_WIDTH),
                  y_spec(SSM_D_INNER), g_spec(0), g_spec(1), g_spec(2),
                  w_spec(BRANCH_WIDTH), w_spec(ATT_OUT_WIDTH), w_spec(SSM_D_INNER)],
        out_specs=pl.BlockSpec((tm, tn), lambda i, j: (i, j)),
        out_shape=jax.ShapeDtypeStruct((m, D_MODEL), BF16),
        compiler_params=_params("parallel", "arbitrary"),
        name="merge",
    )(h, ya, yb, yc, wt_in, wt_in, wt_in, wa, wb, wc)


def _out_proj_norm_kernel(a_ref, w_ref, r_ref, nw_ref, *outs_and_scratch):
    *x_out, h_ref, w_bf = outs_and_scratch

    @pl.when(pl.program_id(0) == 0)
    def _():
        w_bf[...] = w_ref[...].astype(BF16)

    x_new = r_ref[...] + jnp.dot(a_ref[...], w_bf[...], preferred_element_type=F32)
    for x_ref in x_out:
        x_ref[...] = x_new
    ms = jnp.mean(x_new * x_new, axis=-1, keepdims=True)
    h_ref[...] = (x_new * lax.rsqrt(ms + 1e-6) * nw_ref[...]).astype(h_ref.dtype)


def _out_proj_norm(a, w, residual, norm_w, layer, h_dtype):
    m, k = a.shape
    d = D_MODEL
    tm = OUT_TILE_M
    emit_x = h_dtype == BF16
    row_spec = pl.BlockSpec((tm, d), lambda i: (i, 0))
    out_specs = [row_spec] * (2 if emit_x else 1)
    out_shape = [jax.ShapeDtypeStruct((m, d), F32)] * emit_x + [jax.ShapeDtypeStruct((m, d), h_dtype)]
    outs = pl.pallas_call(
        _out_proj_norm_kernel,
        grid=(m // tm,),
        in_specs=[pl.BlockSpec((tm, k), lambda i: (i, 0)),
                  pl.BlockSpec((None, k, d), lambda i: (layer, 0, 0),
                               pipeline_mode=pl.Buffered(1)),
                  row_spec,
                  _layer_rows(layer, d, 1)],
        out_specs=out_specs,
        out_shape=out_shape,
        scratch_shapes=[pltpu.VMEM((k, d), BF16)],
        compiler_params=_params("arbitrary"),
        name="out_proj_norm",
    )(a, w, residual, norm_w)
    return (outs[0], outs[1]) if emit_x else (None, outs[0])


def kernel(x, norm_w, w_in, gmlp_ln_w, gmlp_ln_b, gmlp_ws, gmlp_bs, conv_w, conv_b, dt_bias,
           a_log, d_skip, ssm_norm_w, w_branch_a, w_branch_b, w_branch_c, w_out, final_norm_w):
    b, s, d = x.shape
    assert (s, d) == (SEQ, D_MODEL)
    cos2, sin2 = _rope_tables()
    rows = lambda p: p[:, None, :]
    norm_w3, ln_w3, ln_b3 = rows(norm_w), rows(gmlp_ln_w), rows(gmlp_ln_b)
    conv_b3, ssm_norm_w3 = rows(conv_b), rows(ssm_norm_w)
    bs_t = jnp.swapaxes(gmlp_bs, 1, 2)
    dt_bias3, a_log3 = _pad_lanes(dt_bias), _pad_lanes(a_log)
    d_skip_e = rows(jnp.repeat(d_skip, SSM_HEAD_DIM, axis=1))
    wt_in = jnp.swapaxes(w_in, 1, 2)
    next_norm_w3 = jnp.concatenate([norm_w3[1:], final_norm_w.reshape(1, 1, d)], axis=0)
    xf = x.reshape(b * s, d)
    tm, tn = PROJ_TILE_M, PROJ_TILE_N
    h = _rmsnorm(xf, norm_w3, 0, BF16)
    for i in range(DEPTH):
        last = i == DEPTH - 1
        proj = _matmul_nt(h, wt_in, i, 0, OFF_C + WIDTH_C, tm, tn, name="in_proj")
        ya = _gmlp(proj, ln_w3, ln_b3, gmlp_ws, bs_t, i)
        yb = _attention(proj, cos2, sin2)
        yc = _ssd(proj, conv_w, conv_b3, dt_bias3, a_log3, d_skip_e, ssm_norm_w3, i)
        merged = _merge(h, ya, yb, yc, wt_in, w_branch_a, w_branch_b, w_branch_c, i)
        xf, h = _out_proj_norm(merged, w_out, xf, next_norm_w3, i, x.dtype if last else BF16)
    return h.reshape(b, s, d)
```

```python
import math

import jax
import jax.numpy as jnp
from jax import lax
from jax.experimental import pallas as pl
from jax.experimental.pallas import tpu as pltpu

D_MODEL = 2048
SEQ = 2048
DEPTH = 4
BRANCH_WIDTH = 1536
GMLP_CHUNK = 128
GMLP_GROUPS = 12
HEAD_DIM = 128
HEADS_PER_GROUP = 4
DILATED_GROUPS = ((128, 1), (512, 4), (2048, 16))
ATT_BLOCK = 128
ATT_OUT_WIDTH = HEADS_PER_GROUP * HEAD_DIM
ROPE_THETA = 10000.0
SSM_D_INNER = 1536
SSM_HEAD_DIM = 64
SSM_HEADS = 24
SSM_GROUPS = 4
SSM_D_STATE = 128
SSM_CONV = 4
SSM_CHUNK = 128
SSM_CONV_DIM = SSM_D_INNER + 2 * SSM_GROUPS * SSM_D_STATE
HEAD_PAIRS = SSM_HEADS // 2
PAIRS_PER_GROUP = HEAD_PAIRS // SSM_GROUPS

OFF_A = 0
OFF_B = 3 * BRANCH_WIDTH
OFF_C = OFF_B + 3 * BRANCH_WIDTH + ATT_OUT_WIDTH
OFF_DT = OFF_C + SSM_D_INNER + SSM_CONV_DIM
OFF_GATE = OFF_DT + SSM_HEADS
WIDTH_C = 4608
DT_COL = SSM_D_INNER + SSM_CONV_DIM

LANES = 128
SUBLANES = 8
LOG2_E = 1.4426950408889634
LN_2 = 0.6931471805599453
VMEM_LIMIT = 56 * 1024 * 1024
PROJ_TILE_M, PROJ_TILE_N = 2048, 512
OUT_TILE_M = 512
MERGE_TILE_M, MERGE_TILE_N = 1024, 256

F32 = jnp.float32
BF16 = jnp.bfloat16


def _params(*semantics):
    return pltpu.CompilerParams(dimension_semantics=semantics, vmem_limit_bytes=VMEM_LIMIT)


def _layer_rows(layer, width, n_grid):
    if n_grid == 1:
        return pl.BlockSpec((None, 1, width), lambda i: (layer, 0, 0))
    return pl.BlockSpec((None, 1, width), lambda i, j: (layer, 0, 0))


def _rmsnorm_kernel(x_ref, w_ref, o_ref):
    x = x_ref[...]
    ms = jnp.mean(x * x, axis=-1, keepdims=True)
    o_ref[...] = (x * lax.rsqrt(ms + 1e-6) * w_ref[...]).astype(o_ref.dtype)


def _rmsnorm(x, w, layer, rows=256):
    m, d = x.shape
    return pl.pallas_call(
        _rmsnorm_kernel,
        grid=(m // rows,),
        in_specs=[pl.BlockSpec((rows, d), lambda i: (i, 0)), _layer_rows(layer, d, 1)],
        out_specs=pl.BlockSpec((rows, d), lambda i: (i, 0)),
        out_shape=jax.ShapeDtypeStruct((m, d), BF16),
        compiler_params=_params("parallel"),
        name="rmsnorm",
    )(x, w)


NT_DIMS = (((1,), (1,)), ((), ()))


def _wt_rows_spec(layer, row_off, tn, k):
    assert row_off % SUBLANES == 0 and tn % SUBLANES == 0
    return pl.BlockSpec((pl.Element(1), pl.Element(tn), pl.Element(k)),
                        lambda i, j: (layer, pl.multiple_of(row_off + j * tn, SUBLANES), 0))


def _matmul_nt_kernel(a_ref, wt_ref, o_ref):
    o_ref[...] = lax.dot_general(a_ref[...], wt_ref[0].astype(BF16), NT_DIMS,
                                 preferred_element_type=F32)


def _matmul_nt(a, wt, layer, row_off, n_cols, tm, tn, name):
    m, k = a.shape
    assert n_cols % tn == 0 and m % tm == 0
    return pl.pallas_call(
        _matmul_nt_kernel,
        grid=(m // tm, n_cols // tn),
        in_specs=[pl.BlockSpec((tm, k), lambda i, j: (i, 0)), _wt_rows_spec(layer, row_off, tn, k)],
        out_specs=pl.BlockSpec((tm, tn), lambda i, j: (i, j)),
        out_shape=jax.ShapeDtypeStruct((m, n_cols), F32),
        compiler_params=_params("parallel", "arbitrary"),
        name=name,
    )(a, wt)


GMLP_STEP_ROWS = 4 * GMLP_CHUNK

GELU_K1 = -2.0 * math.sqrt(2.0 / math.pi) * LOG2_E
GELU_K3 = 0.044715 * GELU_K1


def _gelu_tanh(x):
    return x / (1.0 + jnp.exp2(x * (GELU_K1 + GELU_K3 * (x * x))))


def _gmlp_kernel(u_ref, v_ref, z_ref, lnw_ref, lnb_ref, ws_ref, bst_ref, o_ref, wm_ref):
    @pl.when(pl.program_id(0) == 0)
    def _():
        row = lax.broadcasted_iota(jnp.int32, (GMLP_CHUNK, GMLP_CHUNK), 0)
        col = lax.broadcasted_iota(jnp.int32, (GMLP_CHUNK, GMLP_CHUNK), 1)
        for g in range(GMLP_GROUPS):
            wm_ref[g] = jnp.where(col <= row, ws_ref[g], 0.0).astype(BF16)

    for c in range(GMLP_STEP_ROWS // GMLP_CHUNK):
        rows = slice(c * GMLP_CHUNK, (c + 1) * GMLP_CHUNK)
        v = _gelu_tanh(v_ref[rows, :])
        mu = jnp.mean(v, axis=-1, keepdims=True)
        vc = v - mu
        var = jnp.mean(vc * vc, axis=-1, keepdims=True)
        vn = vc * lax.rsqrt(var + 1e-5) * lnw_ref[...] + lnb_ref[...]
        for g in range(GMLP_GROUPS):
            sl = slice(g * LANES, (g + 1) * LANES)
            mixed = jnp.dot(wm_ref[g], vn[:, sl].astype(BF16), preferred_element_type=F32)
            mixed = mixed + bst_ref[:, g:g + 1]
            u = _gelu_tanh(u_ref[rows, sl])
            o_ref[rows, sl] = (u * mixed * jax.nn.silu(z_ref[rows, sl])).astype(o_ref.dtype)


def _gmlp(pa, ln_w, ln_b, ws, bs_t, layer):
    m = pa.shape[0]
    w = BRANCH_WIDTH
    return pl.pallas_call(
        _gmlp_kernel,
        grid=(m // GMLP_STEP_ROWS,),
        in_specs=[pl.BlockSpec((GMLP_STEP_ROWS, w), lambda i: (i, 0)),
                  pl.BlockSpec((GMLP_STEP_ROWS, w), lambda i: (i, 1)),
                  pl.BlockSpec((GMLP_STEP_ROWS, w), lambda i: (i, 2)),
                  _layer_rows(layer, w, 1),
                  _layer_rows(layer, w, 1),
                  pl.BlockSpec((None, GMLP_GROUPS, GMLP_CHUNK, GMLP_CHUNK),
                               lambda i: (layer, 0, 0, 0)),
                  pl.BlockSpec((None, GMLP_CHUNK, GMLP_GROUPS), lambda i: (layer, 0, 0))],
        out_specs=pl.BlockSpec((GMLP_STEP_ROWS, w), lambda i: (i, 0)),
        out_shape=jax.ShapeDtypeStruct((m, w), BF16),
        scratch_shapes=[pltpu.VMEM((GMLP_GROUPS, GMLP_CHUNK, GMLP_CHUNK), BF16)],
        compiler_params=_params("arbitrary"),
        name="gmlp",
    )(pa, pa, pa, ln_w, ln_b, ws, bs_t)


ROPE_ROWS = 256
ATT_UNROLL = 16


def _attn_kernel(q1, q2, q3, k1, k2, k3, v1, v2, v3, z_ref, cos_ref, sin_ref, o_ref,
                 qs1, qs2, qs3, ks1, ks2, ks3, os1, os2, os3, ls1, ls2, ls3,
                 sc1, sc2, sc3, mx1, mx2, mx3):
    scale = LOG2_E / math.sqrt(HEAD_DIM)

    def rope_step(i, carry):
        rows = pl.ds(pl.multiple_of(i * ROPE_ROWS, ROPE_ROWS), ROPE_ROWS)
        c = cos_ref[rows, :]
        s = sin_ref[rows, :]
        for src, dst, mul in ((q1, qs1, scale), (q2, qs2, scale), (q3, qs3, scale),
                              (k1, ks1, None), (k2, ks2, None), (k3, ks3, None)):
            t = src[rows, :]
            partner = pltpu.bitcast((t * s).astype(BF16), jnp.uint32)
            partner = pltpu.bitcast(pltpu.roll(partner, HEAD_DIM // 2, 1), BF16)
            r = t * c + partner.astype(F32)
            dst[rows, :] = r if mul is None else r * mul
        return carry

    for i in range(SEQ // ROPE_ROWS):
        rope_step(i, 0)

    row = lax.broadcasted_iota(jnp.int32, (ATT_BLOCK, ATT_BLOCK), 0)
    col = lax.broadcasted_iota(jnp.int32, (ATT_BLOCK, ATT_BLOCK), 1)
    lower = col <= row
    upper = col >= row
    ones = jnp.ones((ATT_BLOCK, HEAD_DIM), BF16)

    def with_ones(v):
        return jnp.concatenate([v.astype(BF16), ones], axis=1)

    def block_rows(dil, idx):
        def rows_at(start):
            if dil == 1:
                return pl.ds(pl.multiple_of(start, ATT_BLOCK), ATT_BLOCK)
            return pl.ds(start, ATT_BLOCK, stride=dil)

        r = idx % dil
        bi = idx // dil
        cur = rows_at(bi * (ATT_BLOCK * dil) + r)
        prev = rows_at(jnp.maximum(bi - 1, 0) * (ATT_BLOCK * dil) + r)
        return cur, prev, bi > 0

    def scores(qs, ks, sc, mx, dil, idx):
        cur, prev, has_prev = block_rows(dil, idx)
        q = qs[cur, :].astype(BF16)
        s_cur = lax.dot_general(q, ks[cur, :].astype(BF16), NT_DIMS,
                                preferred_element_type=F32)
        s_cur = jnp.where(lower, s_cur, -jnp.inf)
        sc[idx, :, 0:ATT_BLOCK] = s_cur
        if SEQ // dil > ATT_BLOCK:
            s_prev = lax.dot_general(q, ks[prev, :].astype(BF16), NT_DIMS,
                                     preferred_element_type=F32)
            s_prev = jnp.where(jnp.logical_and(upper, has_prev), s_prev, -jnp.inf)
            sc[idx, :, ATT_BLOCK:2 * ATT_BLOCK] = s_prev
            s_cur = jnp.maximum(s_cur, s_prev)
        mx[idx] = jnp.broadcast_to(jnp.max(s_cur, axis=-1, keepdims=True), (ATT_BLOCK, HEAD_DIM))

    def values(v_ref, sc, mx, os, ls, dil, idx):
        cur, prev, _ = block_rows(dil, idx)
        m = mx[idx]
        acc = jnp.dot(jnp.exp2(sc[idx, :, 0:ATT_BLOCK] - m).astype(BF16),
                      with_ones(v_ref[cur, :]), preferred_element_type=F32)
        if SEQ // dil > ATT_BLOCK:
            acc = acc + jnp.dot(jnp.exp2(sc[idx, :, ATT_BLOCK:2 * ATT_BLOCK] - m).astype(BF16),
                                with_ones(v_ref[prev, :]), preferred_element_type=F32)
        l = acc[:, HEAD_DIM:]
        os[cur, :] = acc[:, :HEAD_DIM] / l
        ls[cur, :] = m * LN_2 + jnp.log(l)

    groups = ((qs1, ks1, v1, os1, ls1, sc1, mx1, DILATED_GROUPS[0][1]),
              (qs2, ks2, v2, os2, ls2, sc2, mx2, DILATED_GROUPS[1][1]),
              (qs3, ks3, v3, os3, ls3, sc3, mx3, DILATED_GROUPS[2][1]))
    n_idx = SEQ // ATT_BLOCK

    def scores_step(it, carry):
        for u in range(ATT_UNROLL):
            for qs, ks, _, _, _, sc, mx, dil in groups:
                scores(qs, ks, sc, mx, dil, it * ATT_UNROLL + u)
        return carry

    def values_step(it, carry):
        for u in range(ATT_UNROLL):
            for _, _, v_ref, os, ls, sc, mx, dil in groups:
                values(v_ref, sc, mx, os, ls, dil, it * ATT_UNROLL + u)
        return carry

    lax.fori_loop(0, n_idx // ATT_UNROLL, scores_step, 0)
    lax.fori_loop(0, n_idx // ATT_UNROLL, values_step, 0)

    def combine_step(i, carry):
        rows = pl.ds(pl.multiple_of(i * ROPE_ROWS, ROPE_ROWS), ROPE_ROWS)
        l1 = ls1[rows, :]
        l2 = ls2[rows, :]
        l3 = ls3[rows, :]
        mx = jnp.maximum(jnp.maximum(l1, l2), l3)
        e1 = jnp.exp(l1 - mx)
        e2 = jnp.exp(l2 - mx)
        e3 = jnp.exp(l3 - mx)
        o = (e1 * os1[rows, :] + e2 * os2[rows, :] + e3 * os3[rows, :]) / (e1 + e2 + e3)
        o_ref[rows, :] = (o * jax.nn.silu(z_ref[rows, :])).astype(o_ref.dtype)
        return carry

    lax.fori_loop(0, SEQ // ROPE_ROWS, combine_step, 0)


def _rope_tables():
    inv = 1.0 / (ROPE_THETA ** (jnp.arange(0, HEAD_DIM, 2, dtype=F32) / HEAD_DIM))
    ang = jnp.arange(SEQ, dtype=F32)[:, None] * inv[None, :]
    cos, sin = jnp.cos(ang), jnp.sin(ang)
    return jnp.concatenate([cos, cos], axis=-1), jnp.concatenate([sin, -sin], axis=-1)


def _attention(pb, cos2, sin2):
    m = pb.shape[0]
    batch = m // SEQ
    hpg = HEADS_PER_GROUP
    n_qkv = 3 * hpg
    n_blocks = SEQ // ATT_BLOCK

    first = OFF_B // HEAD_DIM

    def head_spec(seg, grp):
        return pl.BlockSpec((SEQ, HEAD_DIM),
                            lambda b, j: (b, first + seg * n_qkv + grp * hpg + j))

    in_specs = [head_spec(seg, grp) for seg in range(3) for grp in range(3)]
    in_specs.append(pl.BlockSpec((SEQ, HEAD_DIM), lambda b, j: (b, first + 3 * n_qkv + j)))
    in_specs += [pl.BlockSpec((SEQ, HEAD_DIM), lambda b, j: (0, 0))] * 2
    return pl.pallas_call(
        _attn_kernel,
        grid=(batch, hpg),
        in_specs=in_specs,
        out_specs=pl.BlockSpec((SEQ, HEAD_DIM), lambda b, j: (b, j)),
        out_shape=jax.ShapeDtypeStruct((m, ATT_OUT_WIDTH), BF16),
        scratch_shapes=([pltpu.VMEM((SEQ, HEAD_DIM), F32)] * 12
                        + [pltpu.VMEM((n_blocks, ATT_BLOCK, 2 * ATT_BLOCK), F32)] * 2
                        + [pltpu.VMEM((n_blocks, ATT_BLOCK, ATT_BLOCK), F32)] * 4),
        compiler_params=_params("parallel", "parallel"),
        name="dilated_attention",
    )(*([pb] * 10), cos2, sin2)


CONV_HALO = 8
CONV_TILES = SSM_CONV_DIM // LANES


def _ssd_kernel(p_ref, convw_ref, convb_ref, dtb_ref, alog_ref, dskip_ref, normw_ref,
                o_ref, xext, xcs, ys, hstate):
    c = pl.program_id(1)
    L = SSM_CHUNK

    @pl.when(c == 0)
    def _():
        hstate[...] = jnp.zeros_like(hstate)
        xext[:, 0:CONV_HALO, :] = jnp.zeros((CONV_TILES, CONV_HALO, LANES), F32)

    for t in range(CONV_TILES):
        sl = slice(t * LANES, (t + 1) * LANES)
        xext[t, CONV_HALO:CONV_HALO + L, :] = p_ref[:, SSM_D_INNER + t * LANES:
                                                    SSM_D_INNER + (t + 1) * LANES]
        acc = jnp.broadcast_to(convb_ref[:, sl], (L, LANES))
        for k in range(SSM_CONV):
            lo = CONV_HALO - (SSM_CONV - 1) + k
            acc = acc + convw_ref[k:k + 1, sl] * xext[t, lo:lo + L, :]
        xcs[:, sl] = jax.nn.silu(acc)
        xext[t, 0:CONV_HALO, :] = xext[t, L:L + CONV_HALO, :]

    lane = lax.broadcasted_iota(jnp.int32, (L, LANES), 1)
    row = lax.broadcasted_iota(jnp.int32, (L, L), 0)
    col = lax.broadcasted_iota(jnp.int32, (L, L), 1)
    causal = col <= row
    first_head = lane < SSM_HEAD_DIM

    dt = jnp.where(lane < SSM_HEADS,
                   jax.nn.softplus(p_ref[:, DT_COL:DT_COL + LANES] + dtb_ref[...]), 0.0)
    da = dt * (-jnp.exp(alog_ref[...]))
    cum = jnp.dot(causal.astype(F32), da, precision=lax.Precision.HIGHEST,
                  preferred_element_type=F32)
    cum = cum * LOG2_E
    cum_t = cum.T
    dt_t = dt.T
    dsdt_t = jnp.exp2(cum_t[:, L - 1:L] - cum_t) * dt_t
    shift_t = cum_t - jnp.log2(dt_t)

    for g in range(SSM_GROUPS):
        b_off = SSM_D_INNER + g * SSM_D_STATE
        c_off = SSM_D_INNER + SSM_GROUPS * SSM_D_STATE + g * SSM_D_STATE
        bm_t = xcs[:, b_off:b_off + SSM_D_STATE].T
        cm = xcs[:, c_off:c_off + SSM_D_STATE].astype(BF16)
        cb = jnp.dot(cm, bm_t.astype(BF16), preferred_element_type=F32)
        for pj in range(PAIRS_PER_GROUP):
            pair = g * PAIRS_PER_GROUP + pj
            sl = slice(pair * LANES, (pair + 1) * LANES)
            xs = xcs[:, sl]
            xs_b = xs.astype(BF16)
            y_h, st_h, cc_h = [], [], []
            for hh in range(2):
                h = 2 * pair + hh
                ccol = jnp.broadcast_to(cum[:, h:h + 1], (L, L))
                decay_dt = jnp.exp2(jnp.where(causal, ccol - shift_t[h:h + 1, :], -jnp.inf))
                y_h.append(jnp.dot((cb * decay_dt).astype(BF16), xs_b,
                                   preferred_element_type=F32))
                bw = (bm_t * dsdt_t[h:h + 1, :]).astype(BF16)
                st_h.append(jnp.dot(bw, xs_b, preferred_element_type=F32))
                cc_h.append(ccol)
            expcum = jnp.exp2(jnp.where(first_head, cc_h[0], cc_h[1]))
            h_prev = hstate[pair]
            y = jnp.where(first_head, y_h[0], y_h[1])
            y = y + jnp.dot(cm, h_prev.astype(BF16), preferred_element_type=F32) * expcum
            ys[:, sl] = y + dskip_ref[:, sl] * xs
            hstate[pair] = (h_prev * expcum[L - 1:L, :]
                            + jnp.where(first_head, st_h[0], st_h[1]))

    gw = SSM_D_INNER // SSM_GROUPS
    for g in range(SSM_GROUPS):
        sl = slice(g * gw, (g + 1) * gw)
        y = ys[:, sl] * jax.nn.silu(p_ref[:, sl])
        ms = jnp.mean(y * y, axis=-1, keepdims=True)
        o_ref[:, sl] = (y * lax.rsqrt(ms + 1e-5) * normw_ref[:, sl]).astype(o_ref.dtype)


def _pad_lanes(v):
    return jnp.pad(v, ((0, 0), (0, LANES - v.shape[1])))[:, None, :]


def _ssd(pc, conv_w, conv_b, dt_bias, a_log, d_skip_e, norm_w, layer):
    m = pc.shape[0]
    batch = m // SEQ
    nc = SEQ // SSM_CHUNK
    return pl.pallas_call(
        _ssd_kernel,
        grid=(batch, nc),
        in_specs=[pl.BlockSpec((pl.Element(SSM_CHUNK), pl.Element(WIDTH_C)),
                               lambda b, c: (pl.multiple_of((b * nc + c) * SSM_CHUNK, SSM_CHUNK),
                                             OFF_C)),
                  pl.BlockSpec((None, SSM_CONV, SSM_CONV_DIM), lambda b, c: (layer, 0, 0)),
                  _layer_rows(layer, SSM_CONV_DIM, 2),
                  _layer_rows(layer, LANES, 2),
                  _layer_rows(layer, LANES, 2),
                  _layer_rows(layer, SSM_D_INNER, 2),
                  _layer_rows(layer, SSM_D_INNER, 2)],
        out_specs=pl.BlockSpec((SSM_CHUNK, SSM_D_INNER), lambda b, c: (b * nc + c, 0)),
        out_shape=jax.ShapeDtypeStruct((m, SSM_D_INNER), BF16),
        scratch_shapes=[pltpu.VMEM((CONV_TILES, CONV_HALO + SSM_CHUNK, LANES), F32),
                        pltpu.VMEM((SSM_CHUNK, SSM_CONV_DIM), F32),
                        pltpu.VMEM((SSM_CHUNK, SSM_D_INNER), F32),
                        pltpu.VMEM((HEAD_PAIRS, SSM_D_STATE, LANES), F32)],
        compiler_params=_params("parallel", "arbitrary"),
        name="ssd",
    )(pc, conv_w, conv_b, dt_bias, a_log, d_skip_e, norm_w)


def _merge_kernel(h_ref, ya_ref, yb_ref, yc_ref, ga_ref, gb_ref, gc_ref, wa_ref, wb_ref, wc_ref,
                  o_ref):
    h = h_ref[...]

    def branch(y_ref, g_ref, w_ref):
        gate = lax.dot_general(h, g_ref[0].astype(BF16), NT_DIMS, preferred_element_type=F32)
        return jax.nn.sigmoid(gate) * jnp.dot(y_ref[...], w_ref[...].astype(BF16),
                                              preferred_element_type=F32)
    acc = branch(ya_ref, ga_ref, wa_ref) + branch(yb_ref, gb_ref, wb_ref)
    o_ref[...] = (acc + branch(yc_ref, gc_ref, wc_ref)).astype(o_ref.dtype)


def _merge(h, ya, yb, yc, wt_in, wa, wb, wc, layer):
    m = ya.shape[0]
    tm, tn = MERGE_TILE_M, MERGE_TILE_N
    y_spec = lambda w: pl.BlockSpec((tm, w), lambda i, j: (i, 0))
    g_spec = lambda k: _wt_rows_spec(layer, OFF_GATE + k * D_MODEL, tn, D_MODEL)
    w_spec = lambda w: pl.BlockSpec((None, w, tn), lambda i, j: (layer, 0, j))
    return pl.pallas_call(
        _merge_kernel,
        grid=(m // tm, D_MODEL // tn),
        in_specs=[y_spec(D_MODEL), y_spec(BRANCH_WIDTH), y_spec(ATT_OUT_WIDTH),
                  y_spec(SSM_D_INNER), g_spec(0), g_spec(1), g_spec(2),
                  w_spec(BRANCH_WIDTH), w_spec(ATT_OUT_WIDTH), w_spec(SSM_D_INNER)],
        out_specs=pl.BlockSpec((tm, tn), lambda i, j: (i, j)),
        out_shape=jax.ShapeDtypeStruct((m, D_MODEL), BF16),
        compiler_params=_params("parallel", "arbitrary"),
        name="merge",
    )(h, ya, yb, yc, wt_in, wt_in, wt_in, wa, wb, wc)


def _out_proj_norm_kernel(a_ref, w_ref, r_ref, nw_ref, *outs_and_scratch):
    *x_out, h_ref, w_bf = outs_and_scratch

    @pl.when(pl.program_id(0) == 0)
    def _():
        w_bf[...] = w_ref[...].astype(BF16)

    x_new = r_ref[...] + jnp.dot(a_ref[...], w_bf[...], preferred_element_type=F32)
    for x_ref in x_out:
        x_ref[...] = x_new
    ms = jnp.mean(x_new * x_new, axis=-1, keepdims=True)
    h_ref[...] = (x_new * lax.rsqrt(ms + 1e-6) * nw_ref[...]).astype(h_ref.dtype)


def _out_proj_norm(a, w, residual, norm_w, layer, h_dtype):
    m, k = a.shape
    d = D_MODEL
    tm = OUT_TILE_M
    emit_x = h_dtype == BF16
    row_spec = pl.BlockSpec((tm, d), lambda i: (i, 0))
    out_specs = [row_spec] * (2 if emit_x else 1)
    out_shape = [jax.ShapeDtypeStruct((m, d), F32)] * emit_x + [jax.ShapeDtypeStruct((m, d), h_dtype)]
    outs = pl.pallas_call(
        _out_proj_norm_kernel,
        grid=(m // tm,),
        in_specs=[pl.BlockSpec((tm, k), lambda i: (i, 0)),
                  pl.BlockSpec((None, k, d), lambda i: (layer, 0, 0),
                               pipeline_mode=pl.Buffered(1)),
                  row_spec,
                  _layer_rows(layer, d, 1)],
        out_specs=out_specs,
        out_shape=out_shape,
        scratch_shapes=[pltpu.VMEM((k, d), BF16)],
        compiler_params=_params("arbitrary"),
        name="out_proj_norm",
    )(a, w, residual, norm_w)
    return (outs[0], outs[1]) if emit_x else (None, outs[0])


def kernel(x, norm_w, w_in, gmlp_ln_w, gmlp_ln_b, gmlp_ws, gmlp_bs, conv_w, conv_b, dt_bias,
           a_log, d_skip, ssm_norm_w, w_branch_a, w_branch_b, w_branch_c, w_out, final_norm_w):
    b, s, d = x.shape
    assert (s, d) == (SEQ, D_MODEL)
    cos2, sin2 = _rope_tables()
    rows = lambda p: p[:, None, :]
    norm_w3, ln_w3, ln_b3 = rows(norm_w), rows(gmlp_ln_w), rows(gmlp_ln_b)
    conv_b3, ssm_norm_w3 = rows(conv_b), rows(ssm_norm_w)
    bs_t = jnp.swapaxes(gmlp_bs, 1, 2)
    dt_bias3, a_log3 = _pad_lanes(dt_bias), _pad_lanes(a_log)
    d_skip_e = rows(jnp.repeat(d_skip, SSM_HEAD_DIM, axis=1))
    wt_in = jnp.swapaxes(w_in, 1, 2)
    next_norm_w3 = jnp.concatenate([norm_w3[1:], final_norm_w.reshape(1, 1, d)], axis=0)
    xf = x.reshape(b * s, d)
    tm, tn = PROJ_TILE_M, PROJ_TILE_N
    h = _rmsnorm(xf, norm_w3, 0)
    for i in range(DEPTH):
        last = i == DEPTH - 1
        proj = _matmul_nt(h, wt_in, i, OFF_A, OFF_C + WIDTH_C - OFF_A, tm, tn, name="in_proj")
        ya = _gmlp(proj, ln_w3, ln_b3, gmlp_ws, bs_t, i)
        yb = _attention(proj, cos2, sin2)
        yc = _ssd(proj, conv_w, conv_b3, dt_bias3, a_log3, d_skip_e, ssm_norm_w3, i)
        merged = _merge(h, ya, yb, yc, wt_in, w_branch_a, w_branch_b, w_branch_c, i)
        xf, h = _out_proj_norm(merged, w_out, xf, next_norm_w3, i, x.dtype if last else BF16)
    return h.reshape(b, s, d)
```

```python
import math

import jax
import jax.numpy as jnp
from jax import lax
from jax.experimental import pallas as pl
from jax.experimental.pallas import tpu as pltpu

D_MODEL = 2048
SEQ = 2048
DEPTH = 4
BRANCH_WIDTH = 1536
GMLP_CHUNK = 128
GMLP_GROUPS = 12
HEAD_DIM = 128
HEADS_PER_GROUP = 4
DILATED_GROUPS = ((128, 1), (512, 4), (2048, 16))
ATT_BLOCK = 128
ATT_OUT_WIDTH = HEADS_PER_GROUP * HEAD_DIM
ROPE_THETA = 10000.0
SSM_D_INNER = 1536
SSM_HEAD_DIM = 64
SSM_HEADS = 24
SSM_GROUPS = 4
SSM_D_STATE = 128
SSM_CONV = 4
SSM_CHUNK = 128
SSM_CONV_DIM = SSM_D_INNER + 2 * SSM_GROUPS * SSM_D_STATE
HEAD_PAIRS = SSM_HEADS // 2
PAIRS_PER_GROUP = HEAD_PAIRS // SSM_GROUPS

OFF_A = 0
OFF_B = 3 * BRANCH_WIDTH
OFF_C = OFF_B + 3 * BRANCH_WIDTH + ATT_OUT_WIDTH
OFF_DT = OFF_C + SSM_D_INNER + SSM_CONV_DIM
OFF_GATE = OFF_DT + SSM_HEADS
WIDTH_C = 4608
DT_COL = SSM_D_INNER + SSM_CONV_DIM

LANES = 128
SUBLANES = 8
LOG2_E = 1.4426950408889634
LN_2 = 0.6931471805599453
VMEM_LIMIT = 56 * 1024 * 1024
PROJ_TILE_M, PROJ_TILE_N = 2048, 1024
OUT_TILE_M = 512
MERGE_TILE_M, MERGE_TILE_N = 1024, 256

F32 = jnp.float32
BF16 = jnp.bfloat16


def _params(*semantics):
    return pltpu.CompilerParams(dimension_semantics=semantics, vmem_limit_bytes=VMEM_LIMIT)


def _layer_rows(layer, width, n_grid):
    if n_grid == 1:
        return pl.BlockSpec((None, 1, width), lambda i: (layer, 0, 0))
    return pl.BlockSpec((None, 1, width), lambda i, j: (layer, 0, 0))


def _rmsnorm_kernel(x_ref, w_ref, o_ref):
    x = x_ref[...]
    ms = jnp.mean(x * x, axis=-1, keepdims=True)
    o_ref[...] = (x * lax.rsqrt(ms + 1e-6) * w_ref[...]).astype(o_ref.dtype)


def _rmsnorm(x, w, layer, rows=256):
    m, d = x.shape
    return pl.pallas_call(
        _rmsnorm_kernel,
        grid=(m // rows,),
        in_specs=[pl.BlockSpec((rows, d), lambda i: (i, 0)), _layer_rows(layer, d, 1)],
        out_specs=pl.BlockSpec((rows, d), lambda i: (i, 0)),
        out_shape=jax.ShapeDtypeStruct((m, d), BF16),
        compiler_params=_params("parallel"),
        name="rmsnorm",
    )(x, w)


NT_DIMS = (((1,), (1,)), ((), ()))


def _wt_rows_spec(layer, row_off, tn, k):
    assert row_off % SUBLANES == 0 and tn % SUBLANES == 0
    return pl.BlockSpec((pl.Element(1), pl.Element(tn), pl.Element(k)),
                        lambda i, j: (layer, pl.multiple_of(row_off + j * tn, SUBLANES), 0))


def _matmul_nt_kernel(a_ref, wt_ref, o_ref):
    o_ref[...] = lax.dot_general(a_ref[...], wt_ref[0].astype(BF16), NT_DIMS,
                                 preferred_element_type=F32)


def _matmul_nt(a, wt, layer, row_off, n_cols, tm, tn, name):
    m, k = a.shape
    assert n_cols % tn == 0 and m % tm == 0
    return pl.pallas_call(
        _matmul_nt_kernel,
        grid=(m // tm, n_cols // tn),
        in_specs=[pl.BlockSpec((tm, k), lambda i, j: (i, 0), pipeline_mode=pl.Buffered(1)),
                  _wt_rows_spec(layer, row_off, tn, k)],
        out_specs=pl.BlockSpec((tm, tn), lambda i, j: (i, j)),
        out_shape=jax.ShapeDtypeStruct((m, n_cols), F32),
        compiler_params=_params("parallel", "arbitrary"),
        name=name,
    )(a, wt)


GMLP_STEP_ROWS = 4 * GMLP_CHUNK

GELU_K1 = -2.0 * math.sqrt(2.0 / math.pi) * LOG2_E
GELU_K3 = 0.044715 * GELU_K1


def _gelu_tanh(x):
    return x / (1.0 + jnp.exp2(x * (GELU_K1 + GELU_K3 * (x * x))))


def _gmlp_kernel(u_ref, v_ref, z_ref, lnw_ref, lnb_ref, ws_ref, bst_ref, o_ref, wm_ref):
    @pl.when(pl.program_id(0) == 0)
    def _():
        row = lax.broadcasted_iota(jnp.int32, (GMLP_CHUNK, GMLP_CHUNK), 0)
        col = lax.broadcasted_iota(jnp.int32, (GMLP_CHUNK, GMLP_CHUNK), 1)
        for g in range(GMLP_GROUPS):
            wm_ref[g] = jnp.where(col <= row, ws_ref[g], 0.0).astype(BF16)

    for c in range(GMLP_STEP_ROWS // GMLP_CHUNK):
        rows = slice(c * GMLP_CHUNK, (c + 1) * GMLP_CHUNK)
        v = _gelu_tanh(v_ref[rows, :])
        mu = jnp.mean(v, axis=-1, keepdims=True)
        vc = v - mu
        var = jnp.mean(vc * vc, axis=-1, keepdims=True)
        vn = vc * lax.rsqrt(var + 1e-5) * lnw_ref[...] + lnb_ref[...]
        for g in range(GMLP_GROUPS):
            sl = slice(g * LANES, (g + 1) * LANES)
            mixed = jnp.dot(wm_ref[g], vn[:, sl].astype(BF16), preferred_element_type=F32)
            mixed = mixed + bst_ref[:, g:g + 1]
            u = _gelu_tanh(u_ref[rows, sl])
            o_ref[rows, sl] = (u * mixed * jax.nn.silu(z_ref[rows, sl])).astype(o_ref.dtype)


def _gmlp(pa, ln_w, ln_b, ws, bs_t, layer):
    m = pa.shape[0]
    w = BRANCH_WIDTH
    return pl.pallas_call(
        _gmlp_kernel,
        grid=(m // GMLP_STEP_ROWS,),
        in_specs=[pl.BlockSpec((GMLP_STEP_ROWS, w), lambda i: (i, 0)),
                  pl.BlockSpec((GMLP_STEP_ROWS, w), lambda i: (i, 1)),
                  pl.BlockSpec((GMLP_STEP_ROWS, w), lambda i: (i, 2)),
                  _layer_rows(layer, w, 1),
                  _layer_rows(layer, w, 1),
                  pl.BlockSpec((None, GMLP_GROUPS, GMLP_CHUNK, GMLP_CHUNK),
                               lambda i: (layer, 0, 0, 0)),
                  pl.BlockSpec((None, GMLP_CHUNK, GMLP_GROUPS), lambda i: (layer, 0, 0))],
        out_specs=pl.BlockSpec((GMLP_STEP_ROWS, w), lambda i: (i, 0)),
        out_shape=jax.ShapeDtypeStruct((m, w), BF16),
        scratch_shapes=[pltpu.VMEM((GMLP_GROUPS, GMLP_CHUNK, GMLP_CHUNK), BF16)],
        compiler_params=_params("arbitrary"),
        name="gmlp",
    )(pa, pa, pa, ln_w, ln_b, ws, bs_t)


ROPE_ROWS = 256
ATT_UNROLL = 16


def _attn_kernel(q1, q2, q3, k1, k2, k3, v1, v2, v3, z_ref, cos_ref, sin_ref, o_ref,
                 qs1, qs2, qs3, ks1, ks2, ks3, os1, os2, os3, ls1, ls2, ls3,
                 sc1, sc2, sc3, mx1, mx2, mx3):
    scale = LOG2_E / math.sqrt(HEAD_DIM)

    def rope_step(i, carry):
        rows = pl.ds(pl.multiple_of(i * ROPE_ROWS, ROPE_ROWS), ROPE_ROWS)
        c = cos_ref[rows, :]
        s = sin_ref[rows, :]
        for src, dst, mul in ((q1, qs1, scale), (q2, qs2, scale), (q3, qs3, scale),
                              (k1, ks1, None), (k2, ks2, None), (k3, ks3, None)):
            t = src[rows, :]
            partner = pltpu.bitcast((t * s).astype(BF16), jnp.uint32)
            partner = pltpu.bitcast(pltpu.roll(partner, HEAD_DIM // 2, 1), BF16)
            r = t * c + partner.astype(F32)
            dst[rows, :] = r if mul is None else r * mul
        return carry

    for i in range(SEQ // ROPE_ROWS):
        rope_step(i, 0)

    row = lax.broadcasted_iota(jnp.int32, (ATT_BLOCK, ATT_BLOCK), 0)
    col = lax.broadcasted_iota(jnp.int32, (ATT_BLOCK, ATT_BLOCK), 1)
    lower = col <= row
    upper = col >= row
    ones = jnp.ones((ATT_BLOCK, HEAD_DIM), BF16)

    def with_ones(v):
        return jnp.concatenate([v.astype(BF16), ones], axis=1)

    def block_rows(dil, idx):
        def rows_at(start):
            if dil == 1:
                return pl.ds(pl.multiple_of(start, ATT_BLOCK), ATT_BLOCK)
            return pl.ds(start, ATT_BLOCK, stride=dil)

        r = idx % dil
        bi = idx // dil
        cur = rows_at(bi * (ATT_BLOCK * dil) + r)
        prev = rows_at(jnp.maximum(bi - 1, 0) * (ATT_BLOCK * dil) + r)
        return cur, prev, bi > 0

    def scores(qs, ks, sc, mx, dil, idx):
        cur, prev, has_prev = block_rows(dil, idx)
        q = qs[cur, :].astype(BF16)
        s_cur = lax.dot_general(q, ks[cur, :].astype(BF16), NT_DIMS,
                                preferred_element_type=F32)
        s_cur = jnp.where(lower, s_cur, -jnp.inf)
        sc[idx, :, 0:ATT_BLOCK] = s_cur
        if SEQ // dil > ATT_BLOCK:
            s_prev = lax.dot_general(q, ks[prev, :].astype(BF16), NT_DIMS,
                                     preferred_element_type=F32)
            s_prev = jnp.where(jnp.logical_and(upper, has_prev), s_prev, -jnp.inf)
            sc[idx, :, ATT_BLOCK:2 * ATT_BLOCK] = s_prev
            s_cur = jnp.maximum(s_cur, s_prev)
        mx[idx] = jnp.broadcast_to(jnp.max(s_cur, axis=-1, keepdims=True), (ATT_BLOCK, HEAD_DIM))

    def values(v_ref, sc, mx, os, ls, dil, idx):
        cur, prev, _ = block_rows(dil, idx)
        m = mx[idx]
        acc = jnp.dot(jnp.exp2(sc[idx, :, 0:ATT_BLOCK] - m).astype(BF16),
                      with_ones(v_ref[cur, :]), preferred_element_type=F32)
        if SEQ // dil > ATT_BLOCK:
            acc = acc + jnp.dot(jnp.exp2(sc[idx, :, ATT_BLOCK:2 * ATT_BLOCK] - m).astype(BF16),
                                with_ones(v_ref[prev, :]), preferred_element_type=F32)
        l = acc[:, HEAD_DIM:]
        os[cur, :] = acc[:, :HEAD_DIM] / l
        ls[cur, :] = m * LN_2 + jnp.log(l)

    groups = ((qs1, ks1, v1, os1, ls1, sc1, mx1, DILATED_GROUPS[0][1]),
              (qs2, ks2, v2, os2, ls2, sc2, mx2, DILATED_GROUPS[1][1]),
              (qs3, ks3, v3, os3, ls3, sc3, mx3, DILATED_GROUPS[2][1]))
    n_idx = SEQ // ATT_BLOCK

    def scores_step(it, carry):
        for u in range(ATT_UNROLL):
            for qs, ks, _, _, _, sc, mx, dil in groups:
                scores(qs, ks, sc, mx, dil, it * ATT_UNROLL + u)
        return carry

    def values_step(it, carry):
        for u in range(ATT_UNROLL):
            for _, _, v_ref, os, ls, sc, mx, dil in groups:
                values(v_ref, sc, mx, os, ls, dil, it * ATT_UNROLL + u)
        return carry

    lax.fori_loop(0, n_idx // ATT_UNROLL, scores_step, 0)
    lax.fori_loop(0, n_idx // ATT_UNROLL, values_step, 0)

    def combine_step(i, carry):
        rows = pl.ds(pl.multiple_of(i * ROPE_ROWS, ROPE_ROWS), ROPE_ROWS)
        l1 = ls1[rows, :]
        l2 = ls2[rows, :]
        l3 = ls3[rows, :]
        mx = jnp.maximum(jnp.maximum(l1, l2), l3)
        e1 = jnp.exp(l1 - mx)
        e2 = jnp.exp(l2 - mx)
        e3 = jnp.exp(l3 - mx)
        o = (e1 * os1[rows, :] + e2 * os2[rows, :] + e3 * os3[rows, :]) / (e1 + e2 + e3)
        o_ref[rows, :] = (o * jax.nn.silu(z_ref[rows, :])).astype(o_ref.dtype)
        return carry

    lax.fori_loop(0, SEQ // ROPE_ROWS, combine_step, 0)


def _rope_tables():
    inv = 1.0 / (ROPE_THETA ** (jnp.arange(0, HEAD_DIM, 2, dtype=F32) / HEAD_DIM))
    ang = jnp.arange(SEQ, dtype=F32)[:, None] * inv[None, :]
    cos, sin = jnp.cos(ang), jnp.sin(ang)
    return jnp.concatenate([cos, cos], axis=-1), jnp.concatenate([sin, -sin], axis=-1)


def _attention(pb, cos2, sin2):
    m = pb.shape[0]
    batch = m // SEQ
    hpg = HEADS_PER_GROUP
    n_qkv = 3 * hpg
    n_blocks = SEQ // ATT_BLOCK

    first = OFF_B // HEAD_DIM

    def head_spec(seg, grp):
        return pl.BlockSpec((SEQ, HEAD_DIM),
                            lambda b, j: (b, first + seg * n_qkv + grp * hpg + j))

    in_specs = [head_spec(seg, grp) for seg in range(3) for grp in range(3)]
    in_specs.append(pl.BlockSpec((SEQ, HEAD_DIM), lambda b, j: (b, first + 3 * n_qkv + j)))
    in_specs += [pl.BlockSpec((SEQ, HEAD_DIM), lambda b, j: (0, 0))] * 2
    return pl.pallas_call(
        _attn_kernel,
        grid=(batch, hpg),
        in_specs=in_specs,
        out_specs=pl.BlockSpec((SEQ, HEAD_DIM), lambda b, j: (b, j)),
        out_shape=jax.ShapeDtypeStruct((m, ATT_OUT_WIDTH), BF16),
        scratch_shapes=([pltpu.VMEM((SEQ, HEAD_DIM), F32)] * 12
                        + [pltpu.VMEM((n_blocks, ATT_BLOCK, 2 * ATT_BLOCK), F32)] * 2
                        + [pltpu.VMEM((n_blocks, ATT_BLOCK, ATT_BLOCK), F32)] * 4),
        compiler_params=_params("parallel", "parallel"),
        name="dilated_attention",
    )(*([pb] * 10), cos2, sin2)


CONV_HALO = 8
CONV_TILES = SSM_CONV_DIM // LANES


def _ssd_kernel(p_ref, convw_ref, convb_ref, dtb_ref, alog_ref, dskip_ref, normw_ref,
                o_ref, xext, xcs, ys, hstate):
    c = pl.program_id(1)
    L = SSM_CHUNK

    @pl.when(c == 0)
    def _():
        hstate[...] = jnp.zeros_like(hstate)
        xext[:, 0:CONV_HALO, :] = jnp.zeros((CONV_TILES, CONV_HALO, LANES), F32)

    for t in range(CONV_TILES):
        sl = slice(t * LANES, (t + 1) * LANES)
        xext[t, CONV_HALO:CONV_HALO + L, :] = p_ref[:, SSM_D_INNER + t * LANES:
                                                    SSM_D_INNER + (t + 1) * LANES]
        acc = jnp.broadcast_to(convb_ref[:, sl], (L, LANES))
        for k in range(SSM_CONV):
            lo = CONV_HALO - (SSM_CONV - 1) + k
            acc = acc + convw_ref[k:k + 1, sl] * xext[t, lo:lo + L, :]
        xcs[:, sl] = jax.nn.silu(acc)
        xext[t, 0:CONV_HALO, :] = xext[t, L:L + CONV_HALO, :]

    lane = lax.broadcasted_iota(jnp.int32, (L, LANES), 1)
    row = lax.broadcasted_iota(jnp.int32, (L, L), 0)
    col = lax.broadcasted_iota(jnp.int32, (L, L), 1)
    causal = col <= row
    first_head = lane < SSM_HEAD_DIM

    dt = jnp.where(lane < SSM_HEADS,
                   jax.nn.softplus(p_ref[:, DT_COL:DT_COL + LANES] + dtb_ref[...]), 0.0)
    da = dt * (-jnp.exp(alog_ref[...]))
    cum = jnp.dot(causal.astype(F32), da, precision=lax.Precision.HIGHEST,
                  preferred_element_type=F32)
    cum = cum * LOG2_E
    cum_t = cum.T
    dt_t = dt.T
    dsdt_t = jnp.exp2(cum_t[:, L - 1:L] - cum_t) * dt_t
    shift_t = cum_t - jnp.log2(dt_t)

    for g in range(SSM_GROUPS):
        b_off = SSM_D_INNER + g * SSM_D_STATE
        c_off = SSM_D_INNER + SSM_GROUPS * SSM_D_STATE + g * SSM_D_STATE
        bm_t = xcs[:, b_off:b_off + SSM_D_STATE].T
        cm = xcs[:, c_off:c_off + SSM_D_STATE].astype(BF16)
        cb = jnp.dot(cm, bm_t.astype(BF16), preferred_element_type=F32)
        for pj in range(PAIRS_PER_GROUP):
            pair = g * PAIRS_PER_GROUP + pj
            sl = slice(pair * LANES, (pair + 1) * LANES)
            xs = xcs[:, sl]
            xs_b = xs.astype(BF16)
            y_h, st_h, cc_h = [], [], []
            for hh in range(2):
                h = 2 * pair + hh
                ccol = jnp.broadcast_to(cum[:, h:h + 1], (L, L))
                decay_dt = jnp.exp2(jnp.where(causal, ccol - shift_t[h:h + 1, :], -jnp.inf))
                y_h.append(jnp.dot((cb * decay_dt).astype(BF16), xs_b,
                                   preferred_element_type=F32))
                bw = (bm_t * dsdt_t[h:h + 1, :]).astype(BF16)
                st_h.append(jnp.dot(bw, xs_b, preferred_element_type=F32))
                cc_h.append(ccol)
            expcum = jnp.exp2(jnp.where(first_head, cc_h[0], cc_h[1]))
            h_prev = hstate[pair]
            y = jnp.where(first_head, y_h[0], y_h[1])
            y = y + jnp.dot(cm, h_prev.astype(BF16), preferred_element_type=F32) * expcum
            ys[:, sl] = y + dskip_ref[:, sl] * xs
            hstate[pair] = (h_prev * expcum[L - 1:L, :]
                            + jnp.where(first_head, st_h[0], st_h[1]))

    gw = SSM_D_INNER // SSM_GROUPS
    for g in range(SSM_GROUPS):
        sl = slice(g * gw, (g + 1) * gw)
        y = ys[:, sl] * jax.nn.silu(p_ref[:, sl])
        ms = jnp.mean(y * y, axis=-1, keepdims=True)
        o_ref[:, sl] = (y * lax.rsqrt(ms + 1e-5) * normw_ref[:, sl]).astype(o_ref.dtype)


def _pad_lanes(v):
    return jnp.pad(v, ((0, 0), (0, LANES - v.shape[1])))[:, None, :]


def _ssd(pc, conv_w, conv_b, dt_bias, a_log, d_skip_e, norm_w, layer):
    m = pc.shape[0]
    batch = m // SEQ
    nc = SEQ // SSM_CHUNK
    return pl.pallas_call(
        _ssd_kernel,
        grid=(batch, nc),
        in_specs=[pl.BlockSpec((pl.Element(SSM_CHUNK), pl.Element(WIDTH_C)),
                               lambda b, c: (pl.multiple_of((b * nc + c) * SSM_CHUNK, SSM_CHUNK),
                                             OFF_C)),
                  pl.BlockSpec((None, SSM_CONV, SSM_CONV_DIM), lambda b, c: (layer, 0, 0)),
                  _layer_rows(layer, SSM_CONV_DIM, 2),
                  _layer_rows(layer, LANES, 2),
                  _layer_rows(layer, LANES, 2),
                  _layer_rows(layer, SSM_D_INNER, 2),
                  _layer_rows(layer, SSM_D_INNER, 2)],
        out_specs=pl.BlockSpec((SSM_CHUNK, SSM_D_INNER), lambda b, c: (b * nc + c, 0)),
        out_shape=jax.ShapeDtypeStruct((m, SSM_D_INNER), BF16),
        scratch_shapes=[pltpu.VMEM((CONV_TILES, CONV_HALO + SSM_CHUNK, LANES), F32),
                        pltpu.VMEM((SSM_CHUNK, SSM_CONV_DIM), F32),
                        pltpu.VMEM((SSM_CHUNK, SSM_D_INNER), F32),
                        pltpu.VMEM((HEAD_PAIRS, SSM_D_STATE, LANES), F32)],
        compiler_params=_params("parallel", "arbitrary"),
        name="ssd",
    )(pc, conv_w, conv_b, dt_bias, a_log, d_skip_e, norm_w)


def _merge_kernel(h_ref, ya_ref, yb_ref, yc_ref, ga_ref, gb_ref, gc_ref, wa_ref, wb_ref, wc_ref,
                  o_ref):
    h = h_ref[...]

    def branch(y_ref, g_ref, w_ref):
        gate = lax.dot_general(h, g_ref[0].astype(BF16), NT_DIMS, preferred_element_type=F32)
        return jax.nn.sigmoid(gate) * jnp.dot(y_ref[...], w_ref[...].astype(BF16),
                                              preferred_element_type=F32)
    acc = branch(ya_ref, ga_ref, wa_ref) + branch(yb_ref, gb_ref, wb_ref)
    o_ref[...] = (acc + branch(yc_ref, gc_ref, wc_ref)).astype(o_ref.dtype)


def _merge(h, ya, yb, yc, wt_in, wa, wb, wc, layer):
    m = ya.shape[0]
    tm, tn = MERGE_TILE_M, MERGE_TILE_N
    y_spec = lambda w: pl.BlockSpec((tm, w), lambda i, j: (i, 0))
    g_spec = lambda k: _wt_rows_spec(layer, OFF_GATE + k * D_MODEL, tn, D_MODEL)
    w_spec = lambda w: pl.BlockSpec((None, w, tn), lambda i, j: (layer, 0, j))
    return pl.pallas_call(
        _merge_kernel,
        grid=(m // tm, D_MODEL // tn),
        in_specs=[y_spec(D_MODEL), y_spec(BRANCH_WIDTH), y_spec(ATT_OUT_WIDTH),
                  y_spec(SSM_D_INNER), g_spec(0), g_spec(1), g_spec(2),
                  w_spec(BRANCH_WIDTH), w_spec(ATT_OUT_WIDTH), w_spec(SSM_D_INNER)],
        out_specs=pl.BlockSpec((tm, tn), lambda i, j: (i, j)),
        out_shape=jax.ShapeDtypeStruct((m, D_MODEL), BF16),
        compiler_params=_params("parallel", "arbitrary"),
        name="merge",
    )(h, ya, yb, yc, wt_in, wt_in, wt_in, wa, wb, wc)


def _out_proj_norm_kernel(a_ref, w_ref, r_ref, nw_ref, *outs_and_scratch):
    *x_out, h_ref, w_bf = outs_and_scratch

    @pl.when(pl.program_id(0) == 0)
    def _():
        w_bf[...] = w_ref[...].astype(BF16)

    x_new = r_ref[...] + jnp.dot(a_ref[...], w_bf[...], preferred_element_type=F32)
    for x_ref in x_out:
        x_ref[...] = x_new
    ms = jnp.mean(x_new * x_new, axis=-1, keepdims=True)
    h_ref[...] = (x_new * lax.rsqrt(ms + 1e-6) * nw_ref[...]).astype(h_ref.dtype)


def _out_proj_norm(a, w, residual, norm_w, layer, h_dtype):
    m, k = a.shape
    d = D_MODEL
    tm = OUT_TILE_M
    emit_x = h_dtype == BF16
    row_spec = pl.BlockSpec((tm, d), lambda i: (i, 0))
    out_specs = [row_spec] * (2 if emit_x else 1)
    out_shape = [jax.ShapeDtypeStruct((m, d), F32)] * emit_x + [jax.ShapeDtypeStruct((m, d), h_dtype)]
    outs = pl.pallas_call(
        _out_proj_norm_kernel,
        grid=(m // tm,),
        in_specs=[pl.BlockSpec((tm, k), lambda i: (i, 0)),
                  pl.BlockSpec((None, k, d), lambda i: (layer, 0, 0),
                               pipeline_mode=pl.Buffered(1)),
                  row_spec,
                  _layer_rows(layer, d, 1)],
        out_specs=out_specs,
        out_shape=out_shape,
        scratch_shapes=[pltpu.VMEM((k, d), BF16)],
        compiler_params=_params("arbitrary"),
        name="out_proj_norm",
    )(a, w, residual, norm_w)
    return (outs[0], outs[1]) if emit_x else (None, outs[0])


def kernel(x, norm_w, w_in, gmlp_ln_w, gmlp_ln_b, gmlp_ws, gmlp_bs, conv_w, conv_b, dt_bias,
           a_log, d_skip, ssm_norm_w, w_branch_a, w_branch_b, w_branch_c, w_out, final_norm_w):
    b, s, d = x.shape
    assert (s, d) == (SEQ, D_MODEL)
    cos2, sin2 = _rope_tables()
    rows = lambda p: p[:, None, :]
    norm_w3, ln_w3, ln_b3 = rows(norm_w), rows(gmlp_ln_w), rows(gmlp_ln_b)
    conv_b3, ssm_norm_w3 = rows(conv_b), rows(ssm_norm_w)
    bs_t = jnp.swapaxes(gmlp_bs, 1, 2)
    dt_bias3, a_log3 = _pad_lanes(dt_bias), _pad_lanes(a_log)
    d_skip_e = rows(jnp.repeat(d_skip, SSM_HEAD_DIM, axis=1))
    wt_in = jnp.swapaxes(w_in, 1, 2)
    next_norm_w3 = jnp.concatenate([norm_w3[1:], final_norm_w.reshape(1, 1, d)], axis=0)
    xf = x.reshape(b * s, d)
    tm, tn = PROJ_TILE_M, PROJ_TILE_N
    h = _rmsnorm(xf, norm_w3, 0)
    for i in range(DEPTH):
        last = i == DEPTH - 1
        proj = _matmul_nt(h, wt_in, i, OFF_A, OFF_C + WIDTH_C - OFF_A, tm, tn, name="in_proj")
        ya = _gmlp(proj, ln_w3, ln_b3, gmlp_ws, bs_t, i)
        yb = _attention(proj, cos2, sin2)
        yc = _ssd(proj, conv_w, conv_b3, dt_bias3, a_log3, d_skip_e, ssm_norm_w3, i)
        merged = _merge(h, ya, yb, yc, wt_in, w_branch_a, w_branch_b, w_branch_c, i)
        xf, h = _out_proj_norm(merged, w_out, xf, next_norm_w3, i, x.dtype if last else BF16)
    return h.reshape(b, s, d)
```

```python
import math

import jax
import jax.numpy as jnp
from jax import lax
from jax.experimental import pallas as pl
from jax.experimental.pallas import tpu as pltpu

D_MODEL = 2048
SEQ = 2048
DEPTH = 4
BRANCH_WIDTH = 1536
GMLP_CHUNK = 128
GMLP_GROUPS = 12
HEAD_DIM = 128
HEADS_PER_GROUP = 4
DILATED_GROUPS = ((128, 1), (512, 4), (2048, 16))
ATT_BLOCK = 128
ATT_OUT_WIDTH = HEADS_PER_GROUP * HEAD_DIM
ROPE_THETA = 10000.0
SSM_D_INNER = 1536
SSM_HEAD_DIM = 64
SSM_HEADS = 24
SSM_GROUPS = 4
SSM_D_STATE = 128
SSM_CONV = 4
SSM_CHUNK = 128
SSM_CONV_DIM = SSM_D_INNER + 2 * SSM_GROUPS * SSM_D_STATE
HEAD_PAIRS = SSM_HEADS // 2
PAIRS_PER_GROUP = HEAD_PAIRS // SSM_GROUPS

OFF_A = 0
OFF_B = 3 * BRANCH_WIDTH
OFF_C = OFF_B + 3 * BRANCH_WIDTH + ATT_OUT_WIDTH
OFF_DT = OFF_C + SSM_D_INNER + SSM_CONV_DIM
OFF_GATE = OFF_DT + SSM_HEADS
WIDTH_C = 4608
DT_COL = SSM_D_INNER + SSM_CONV_DIM

LANES = 128
SUBLANES = 8
LOG2_E = 1.4426950408889634
LN_2 = 0.6931471805599453
VMEM_LIMIT = 56 * 1024 * 1024
PROJ_TILE_M, PROJ_TILE_N = 2048, 1024
OUT_TILE_M = 512
MERGE_TILE_M, MERGE_TILE_N = 1024, 256

F32 = jnp.float32
BF16 = jnp.bfloat16


def _params(*semantics):
    return pltpu.CompilerParams(dimension_semantics=semantics, vmem_limit_bytes=VMEM_LIMIT)


def _layer_rows(layer, width, n_grid):
    if n_grid == 1:
        return pl.BlockSpec((None, 1, width), lambda i: (layer, 0, 0))
    return pl.BlockSpec((None, 1, width), lambda i, j: (layer, 0, 0))


def _rmsnorm_kernel(x_ref, w_ref, o_ref):
    x = x_ref[...]
    ms = jnp.mean(x * x, axis=-1, keepdims=True)
    o_ref[...] = (x * lax.rsqrt(ms + 1e-6) * w_ref[...]).astype(o_ref.dtype)


def _rmsnorm(x, w, layer, rows=256):
    m, d = x.shape
    return pl.pallas_call(
        _rmsnorm_kernel,
        grid=(m // rows,),
        in_specs=[pl.BlockSpec((rows, d), lambda i: (i, 0)), _layer_rows(layer, d, 1)],
        out_specs=pl.BlockSpec((rows, d), lambda i: (i, 0)),
        out_shape=jax.ShapeDtypeStruct((m, d), BF16),
        compiler_params=_params("parallel"),
        name="rmsnorm",
    )(x, w)


NT_DIMS = (((1,), (1,)), ((), ()))


def _wt_rows_spec(layer, row_off, tn, k):
    assert row_off % SUBLANES == 0 and tn % SUBLANES == 0
    return pl.BlockSpec((pl.Element(1), pl.Element(tn), pl.Element(k)),
                        lambda i, j: (layer, pl.multiple_of(row_off + j * tn, SUBLANES), 0))


def _matmul_nt_kernel(a_ref, wt_ref, o_ref):
    o_ref[...] = lax.dot_general(a_ref[...], wt_ref[0].astype(BF16), NT_DIMS,
                                 preferred_element_type=F32)


def _matmul_nt(a, wt, layer, row_off, n_cols, tm, tn, name):
    m, k = a.shape
    assert n_cols % tn == 0 and m % tm == 0
    return pl.pallas_call(
        _matmul_nt_kernel,
        grid=(m // tm, n_cols // tn),
        in_specs=[pl.BlockSpec((tm, k), lambda i, j: (i, 0), pipeline_mode=pl.Buffered(1)),
                  _wt_rows_spec(layer, row_off, tn, k)],
        out_specs=pl.BlockSpec((tm, tn), lambda i, j: (i, j)),
        out_shape=jax.ShapeDtypeStruct((m, n_cols), F32),
        compiler_params=_params("parallel", "arbitrary"),
        name=name,
    )(a, wt)


GMLP_STEP_ROWS = 8 * GMLP_CHUNK

GELU_K1 = -2.0 * math.sqrt(2.0 / math.pi) * LOG2_E
GELU_K3 = 0.044715 * GELU_K1


def _gelu_tanh(x):
    return x / (1.0 + jnp.exp2(x * (GELU_K1 + GELU_K3 * (x * x))))


def _gmlp_kernel(u_ref, v_ref, z_ref, lnw_ref, lnb_ref, ws_ref, bst_ref, o_ref, wm_ref):
    @pl.when(pl.program_id(0) == 0)
    def _():
        row = lax.broadcasted_iota(jnp.int32, (GMLP_CHUNK, GMLP_CHUNK), 0)
        col = lax.broadcasted_iota(jnp.int32, (GMLP_CHUNK, GMLP_CHUNK), 1)
        for g in range(GMLP_GROUPS):
            wm_ref[g] = jnp.where(col <= row, ws_ref[g], 0.0).astype(BF16)

    for c in range(GMLP_STEP_ROWS // GMLP_CHUNK):
        rows = slice(c * GMLP_CHUNK, (c + 1) * GMLP_CHUNK)
        v = _gelu_tanh(v_ref[rows, :])
        mu = jnp.mean(v, axis=-1, keepdims=True)
        vc = v - mu
        var = jnp.mean(vc * vc, axis=-1, keepdims=True)
        vn = vc * lax.rsqrt(var + 1e-5) * lnw_ref[...] + lnb_ref[...]
        for g in range(GMLP_GROUPS):
            sl = slice(g * LANES, (g + 1) * LANES)
            mixed = jnp.dot(wm_ref[g], vn[:, sl].astype(BF16), preferred_element_type=F32)
            mixed = mixed + bst_ref[:, g:g + 1]
            u = _gelu_tanh(u_ref[rows, sl])
            o_ref[rows, sl] = (u * mixed * jax.nn.silu(z_ref[rows, sl])).astype(o_ref.dtype)


def _gmlp(pa, ln_w, ln_b, ws, bs_t, layer):
    m = pa.shape[0]
    w = BRANCH_WIDTH
    return pl.pallas_call(
        _gmlp_kernel,
        grid=(m // GMLP_STEP_ROWS,),
        in_specs=[pl.BlockSpec((GMLP_STEP_ROWS, w), lambda i: (i, 0)),
                  pl.BlockSpec((GMLP_STEP_ROWS, w), lambda i: (i, 1)),
                  pl.BlockSpec((GMLP_STEP_ROWS, w), lambda i: (i, 2)),
                  _layer_rows(layer, w, 1),
                  _layer_rows(layer, w, 1),
                  pl.BlockSpec((None, GMLP_GROUPS, GMLP_CHUNK, GMLP_CHUNK),
                               lambda i: (layer, 0, 0, 0)),
                  pl.BlockSpec((None, GMLP_CHUNK, GMLP_GROUPS), lambda i: (layer, 0, 0))],
        out_specs=pl.BlockSpec((GMLP_STEP_ROWS, w), lambda i: (i, 0)),
        out_shape=jax.ShapeDtypeStruct((m, w), BF16),
        scratch_shapes=[pltpu.VMEM((GMLP_GROUPS, GMLP_CHUNK, GMLP_CHUNK), BF16)],
        compiler_params=_params("arbitrary"),
        name="gmlp",
    )(pa, pa, pa, ln_w, ln_b, ws, bs_t)


ROPE_ROWS = 256
ATT_UNROLL = 16


def _attn_kernel(q1, q2, q3, k1, k2, k3, v1, v2, v3, z_ref, cos_ref, sin_ref, o_ref,
                 qs1, qs2, qs3, ks1, ks2, ks3, os1, os2, os3, ls1, ls2, ls3,
                 sc1, sc2, sc3, mx1, mx2, mx3):
    scale = LOG2_E / math.sqrt(HEAD_DIM)

    def rope_step(i, carry):
        rows = pl.ds(pl.multiple_of(i * ROPE_ROWS, ROPE_ROWS), ROPE_ROWS)
        c = cos_ref[rows, :]
        s = sin_ref[rows, :]
        for src, dst, mul in ((q1, qs1, scale), (q2, qs2, scale), (q3, qs3, scale),
                              (k1, ks1, None), (k2, ks2, None), (k3, ks3, None)):
            t = src[rows, :]
            partner = pltpu.bitcast((t * s).astype(BF16), jnp.uint32)
            partner = pltpu.bitcast(pltpu.roll(partner, HEAD_DIM // 2, 1), BF16)
            r = t * c + partner.astype(F32)
            dst[rows, :] = r if mul is None else r * mul
        return carry

    for i in range(SEQ // ROPE_ROWS):
        rope_step(i, 0)

    row = lax.broadcasted_iota(jnp.int32, (ATT_BLOCK, ATT_BLOCK), 0)
    col = lax.broadcasted_iota(jnp.int32, (ATT_BLOCK, ATT_BLOCK), 1)
    lower = col <= row
    upper = col >= row
    ones = jnp.ones((ATT_BLOCK, HEAD_DIM), BF16)

    def with_ones(v):
        return jnp.concatenate([v.astype(BF16), ones], axis=1)

    def block_rows(dil, idx):
        def rows_at(start):
            if dil == 1:
                return pl.ds(pl.multiple_of(start, ATT_BLOCK), ATT_BLOCK)
            return pl.ds(start, ATT_BLOCK, stride=dil)

        r = idx % dil
        bi = idx // dil
        cur = rows_at(bi * (ATT_BLOCK * dil) + r)
        prev = rows_at(jnp.maximum(bi - 1, 0) * (ATT_BLOCK * dil) + r)
        return cur, prev, bi > 0

    def scores(qs, ks, sc, mx, dil, idx):
        cur, prev, has_prev = block_rows(dil, idx)
        q = qs[cur, :].astype(BF16)
        s_cur = lax.dot_general(q, ks[cur, :].astype(BF16), NT_DIMS,
                                preferred_element_type=F32)
        s_cur = jnp.where(lower, s_cur, -jnp.inf)
        sc[idx, :, 0:ATT_BLOCK] = s_cur
        if SEQ // dil > ATT_BLOCK:
            s_prev = lax.dot_general(q, ks[prev, :].astype(BF16), NT_DIMS,
                                     preferred_element_type=F32)
            s_prev = jnp.where(jnp.logical_and(upper, has_prev), s_prev, -jnp.inf)
            sc[idx, :, ATT_BLOCK:2 * ATT_BLOCK] = s_prev
            s_cur = jnp.maximum(s_cur, s_prev)
        mx[idx] = jnp.broadcast_to(jnp.max(s_cur, axis=-1, keepdims=True), (ATT_BLOCK, HEAD_DIM))

    def values(v_ref, sc, mx, os, ls, dil, idx):
        cur, prev, _ = block_rows(dil, idx)
        m = mx[idx]
        acc = jnp.dot(jnp.exp2(sc[idx, :, 0:ATT_BLOCK] - m).astype(BF16),
                      with_ones(v_ref[cur, :]), preferred_element_type=F32)
        if SEQ // dil > ATT_BLOCK:
            acc = acc + jnp.dot(jnp.exp2(sc[idx, :, ATT_BLOCK:2 * ATT_BLOCK] - m).astype(BF16),
                                with_ones(v_ref[prev, :]), preferred_element_type=F32)
        l = acc[:, HEAD_DIM:]
        os[cur, :] = acc[:, :HEAD_DIM] / l
        ls[cur, :] = m * LN_2 + jnp.log(l)

    groups = ((qs1, ks1, v1, os1, ls1, sc1, mx1, DILATED_GROUPS[0][1]),
              (qs2, ks2, v2, os2, ls2, sc2, mx2, DILATED_GROUPS[1][1]),
              (qs3, ks3, v3, os3, ls3, sc3, mx3, DILATED_GROUPS[2][1]))
    n_idx = SEQ // ATT_BLOCK

    def scores_step(it, carry):
        for u in range(ATT_UNROLL):
            for qs, ks, _, _, _, sc, mx, dil in groups:
                scores(qs, ks, sc, mx, dil, it * ATT_UNROLL + u)
        return carry

    def values_step(it, carry):
        for u in range(ATT_UNROLL):
            for _, _, v_ref, os, ls, sc, mx, dil in groups:
                values(v_ref, sc, mx, os, ls, dil, it * ATT_UNROLL + u)
        return carry

    lax.fori_loop(0, n_idx // ATT_UNROLL, scores_step, 0)
    lax.fori_loop(0, n_idx // ATT_UNROLL, values_step, 0)

    def combine_step(i, carry):
        rows = pl.ds(pl.multiple_of(i * ROPE_ROWS, ROPE_ROWS), ROPE_ROWS)
        l1 = ls1[rows, :]
        l2 = ls2[rows, :]
        l3 = ls3[rows, :]
        mx = jnp.maximum(jnp.maximum(l1, l2), l3)
        e1 = jnp.exp(l1 - mx)
        e2 = jnp.exp(l2 - mx)
        e3 = jnp.exp(l3 - mx)
        o = (e1 * os1[rows, :] + e2 * os2[rows, :] + e3 * os3[rows, :]) / (e1 + e2 + e3)
        o_ref[rows, :] = (o * jax.nn.silu(z_ref[rows, :])).astype(o_ref.dtype)
        return carry

    lax.fori_loop(0, SEQ // ROPE_ROWS, combine_step, 0)


def _rope_tables():
    inv = 1.0 / (ROPE_THETA ** (jnp.arange(0, HEAD_DIM, 2, dtype=F32) / HEAD_DIM))
    ang = jnp.arange(SEQ, dtype=F32)[:, None] * inv[None, :]
    cos, sin = jnp.cos(ang), jnp.sin(ang)
    return jnp.concatenate([cos, cos], axis=-1), jnp.concatenate([sin, -sin], axis=-1)


def _attention(pb, cos2, sin2):
    m = pb.shape[0]
    batch = m // SEQ
    hpg = HEADS_PER_GROUP
    n_qkv = 3 * hpg
    n_blocks = SEQ // ATT_BLOCK

    first = OFF_B // HEAD_DIM

    def head_spec(seg, grp):
        return pl.BlockSpec((SEQ, HEAD_DIM),
                            lambda b, j: (b, first + seg * n_qkv + grp * hpg + j))

    in_specs = [head_spec(seg, grp) for seg in range(3) for grp in range(3)]
    in_specs.append(pl.BlockSpec((SEQ, HEAD_DIM), lambda b, j: (b, first + 3 * n_qkv + j)))
    in_specs += [pl.BlockSpec((SEQ, HEAD_DIM), lambda b, j: (0, 0))] * 2
    return pl.pallas_call(
        _attn_kernel,
        grid=(batch, hpg),
        in_specs=in_specs,
        out_specs=pl.BlockSpec((SEQ, HEAD_DIM), lambda b, j: (b, j)),
        out_shape=jax.ShapeDtypeStruct((m, ATT_OUT_WIDTH), BF16),
        scratch_shapes=([pltpu.VMEM((SEQ, HEAD_DIM), F32)] * 12
                        + [pltpu.VMEM((n_blocks, ATT_BLOCK, 2 * ATT_BLOCK), F32)] * 2
                        + [pltpu.VMEM((n_blocks, ATT_BLOCK, ATT_BLOCK), F32)] * 4),
        compiler_params=_params("parallel", "parallel"),
        name="dilated_attention",
    )(*([pb] * 10), cos2, sin2)


CONV_HALO = 8
CONV_TILES = SSM_CONV_DIM // LANES


def _ssd_kernel(p_ref, convw_ref, convb_ref, dtb_ref, alog_ref, dskip_ref, normw_ref,
                o_ref, xext, xcs, ys, hstate):
    c = pl.program_id(1)
    L = SSM_CHUNK

    @pl.when(c == 0)
    def _():
        hstate[...] = jnp.zeros_like(hstate)
        xext[:, 0:CONV_HALO, :] = jnp.zeros((CONV_TILES, CONV_HALO, LANES), F32)

    for t in range(CONV_TILES):
        sl = slice(t * LANES, (t + 1) * LANES)
        xext[t, CONV_HALO:CONV_HALO + L, :] = p_ref[:, SSM_D_INNER + t * LANES:
                                                    SSM_D_INNER + (t + 1) * LANES]
        acc = jnp.broadcast_to(convb_ref[:, sl], (L, LANES))
        for k in range(SSM_CONV):
            lo = CONV_HALO - (SSM_CONV - 1) + k
            acc = acc + convw_ref[k:k + 1, sl] * xext[t, lo:lo + L, :]
        xcs[:, sl] = jax.nn.silu(acc)
        xext[t, 0:CONV_HALO, :] = xext[t, L:L + CONV_HALO, :]

    lane = lax.broadcasted_iota(jnp.int32, (L, LANES), 1)
    row = lax.broadcasted_iota(jnp.int32, (L, L), 0)
    col = lax.broadcasted_iota(jnp.int32, (L, L), 1)
    causal = col <= row
    first_head = lane < SSM_HEAD_DIM

    dt = jnp.where(lane < SSM_HEADS,
                   jax.nn.softplus(p_ref[:, DT_COL:DT_COL + LANES] + dtb_ref[...]), 0.0)
    da = dt * (-jnp.exp(alog_ref[...]))
    cum = jnp.dot(causal.astype(F32), da, precision=lax.Precision.HIGHEST,
                  preferred_element_type=F32)
    cum = cum * LOG2_E
    cum_t = cum.T
    dt_t = dt.T
    dsdt_t = jnp.exp2(cum_t[:, L - 1:L] - cum_t) * dt_t
    shift_t = cum_t - jnp.log2(dt_t)

    for g in range(SSM_GROUPS):
        b_off = SSM_D_INNER + g * SSM_D_STATE
        c_off = SSM_D_INNER + SSM_GROUPS * SSM_D_STATE + g * SSM_D_STATE
        bm_t = xcs[:, b_off:b_off + SSM_D_STATE].T
        cm = xcs[:, c_off:c_off + SSM_D_STATE].astype(BF16)
        cb = jnp.dot(cm, bm_t.astype(BF16), preferred_element_type=F32)
        for pj in range(PAIRS_PER_GROUP):
            pair = g * PAIRS_PER_GROUP + pj
            sl = slice(pair * LANES, (pair + 1) * LANES)
            xs = xcs[:, sl]
            xs_b = xs.astype(BF16)
            y_h, st_h, cc_h = [], [], []
            for hh in range(2):
                h = 2 * pair + hh
                ccol = jnp.broadcast_to(cum[:, h:h + 1], (L, L))
                decay_dt = jnp.exp2(jnp.where(causal, ccol - shift_t[h:h + 1, :], -jnp.inf))
                y_h.append(jnp.dot((cb * decay_dt).astype(BF16), xs_b,
                                   preferred_element_type=F32))
                bw = (bm_t * dsdt_t[h:h + 1, :]).astype(BF16)
                st_h.append(jnp.dot(bw, xs_b, preferred_element_type=F32))
                cc_h.append(ccol)
            expcum = jnp.exp2(jnp.where(first_head, cc_h[0], cc_h[1]))
            h_prev = hstate[pair]
            y = jnp.where(first_head, y_h[0], y_h[1])
            y = y + jnp.dot(cm, h_prev.astype(BF16), preferred_element_type=F32) * expcum
            ys[:, sl] = y + dskip_ref[:, sl] * xs
            hstate[pair] = (h_prev * expcum[L - 1:L, :]
                            + jnp.where(first_head, st_h[0], st_h[1]))

    gw = SSM_D_INNER // SSM_GROUPS
    for g in range(SSM_GROUPS):
        sl = slice(g * gw, (g + 1) * gw)
        y = ys[:, sl] * jax.nn.silu(p_ref[:, sl])
        ms = jnp.mean(y * y, axis=-1, keepdims=True)
        o_ref[:, sl] = (y * lax.rsqrt(ms + 1e-5) * normw_ref[:, sl]).astype(o_ref.dtype)


def _pad_lanes(v):
    return jnp.pad(v, ((0, 0), (0, LANES - v.shape[1])))[:, None, :]


def _ssd(pc, conv_w, conv_b, dt_bias, a_log, d_skip_e, norm_w, layer):
    m = pc.shape[0]
    batch = m // SEQ
    nc = SEQ // SSM_CHUNK
    return pl.pallas_call(
        _ssd_kernel,
        grid=(batch, nc),
        in_specs=[pl.BlockSpec((pl.Element(SSM_CHUNK), pl.Element(WIDTH_C)),
                               lambda b, c: (pl.multiple_of((b * nc + c) * SSM_CHUNK, SSM_CHUNK),
                                             OFF_C)),
                  pl.BlockSpec((None, SSM_CONV, SSM_CONV_DIM), lambda b, c: (layer, 0, 0)),
                  _layer_rows(layer, SSM_CONV_DIM, 2),
                  _layer_rows(layer, LANES, 2),
                  _layer_rows(layer, LANES, 2),
                  _layer_rows(layer, SSM_D_INNER, 2),
                  _layer_rows(layer, SSM_D_INNER, 2)],
        out_specs=pl.BlockSpec((SSM_CHUNK, SSM_D_INNER), lambda b, c: (b * nc + c, 0)),
        out_shape=jax.ShapeDtypeStruct((m, SSM_D_INNER), BF16),
        scratch_shapes=[pltpu.VMEM((CONV_TILES, CONV_HALO + SSM_CHUNK, LANES), F32),
                        pltpu.VMEM((SSM_CHUNK, SSM_CONV_DIM), F32),
                        pltpu.VMEM((SSM_CHUNK, SSM_D_INNER), F32),
                        pltpu.VMEM((HEAD_PAIRS, SSM_D_STATE, LANES), F32)],
        compiler_params=_params("parallel", "arbitrary"),
        name="ssd",
    )(pc, conv_w, conv_b, dt_bias, a_log, d_skip_e, norm_w)


def _merge_kernel(h_ref, ya_ref, yb_ref, yc_ref, ga_ref, gb_ref, gc_ref, wa_ref, wb_ref, wc_ref,
                  o_ref):
    h = h_ref[...]

    def branch(y_ref, g_ref, w_ref):
        gate = lax.dot_general(h, g_ref[0].astype(BF16), NT_DIMS, preferred_element_type=F32)
        return jax.nn.sigmoid(gate) * jnp.dot(y_ref[...], w_ref[...].astype(BF16),
                                              preferred_element_type=F32)
    acc = branch(ya_ref, ga_ref, wa_ref) + branch(yb_ref, gb_ref, wb_ref)
    o_ref[...] = (acc + branch(yc_ref, gc_ref, wc_ref)).astype(o_ref.dtype)


def _merge(h, ya, yb, yc, wt_in, wa, wb, wc, layer):
    m = ya.shape[0]
    tm, tn = MERGE_TILE_M, MERGE_TILE_N
    y_spec = lambda w: pl.BlockSpec((tm, w), lambda i, j: (i, 0))
    g_spec = lambda k: _wt_rows_spec(layer, OFF_GATE + k * D_MODEL, tn, D_MODEL)
    w_spec = lambda w: pl.BlockSpec((None, w, tn), lambda i, j: (layer, 0, j))
    return pl.pallas_call(
        _merge_kernel,
        grid=(m // tm, D_MODEL // tn),
        in_specs=[y_spec(D_MODEL), y_spec(BRANCH_WIDTH), y_spec(ATT_OUT_WIDTH),
                  y_spec(SSM_D_INNER), g_spec(0), g_spec(1), g_spec(2),
                  w_spec(BRANCH_WIDTH), w_spec(ATT_OUT_WIDTH), w_spec(SSM_D_INNER)],
        out_specs=pl.BlockSpec((tm, tn), lambda i, j: (i, j)),
        out_shape=jax.ShapeDtypeStruct((m, D_MODEL), BF16),
        compiler_params=_params("parallel", "arbitrary"),
        name="merge",
    )(h, ya, yb, yc, wt_in, wt_in, wt_in, wa, wb, wc)


def _out_proj_norm_kernel(a_ref, w_ref, r_ref, nw_ref, *outs_and_scratch):
    *x_out, h_ref, w_bf = outs_and_scratch

    @pl.when(pl.program_id(0) == 0)
    def _():
        w_bf[...] = w_ref[...].astype(BF16)

    x_new = r_ref[...] + jnp.dot(a_ref[...], w_bf[...], preferred_element_type=F32)
    for x_ref in x_out:
        x_ref[...] = x_new
    ms = jnp.mean(x_new * x_new, axis=-1, keepdims=True)
    h_ref[...] = (x_new * lax.rsqrt(ms + 1e-6) * nw_ref[...]).astype(h_ref.dtype)


def _out_proj_norm(a, w, residual, norm_w, layer, h_dtype):
    m, k = a.shape
    d = D_MODEL
    tm = OUT_TILE_M
    emit_x = h_dtype == BF16
    row_spec = pl.BlockSpec((tm, d), lambda i: (i, 0))
    out_specs = [row_spec] * (2 if emit_x else 1)
    out_shape = [jax.ShapeDtypeStruct((m, d), F32)] * emit_x + [jax.ShapeDtypeStruct((m, d), h_dtype)]
    outs = pl.pallas_call(
        _out_proj_norm_kernel,
        grid=(m // tm,),
        in_specs=[pl.BlockSpec((tm, k), lambda i: (i, 0)),
                  pl.BlockSpec((None, k, d), lambda i: (layer, 0, 0),
                               pipeline_mode=pl.Buffered(1)),
                  row_spec,
                  _layer_rows(layer, d, 1)],
        out_specs=out_specs,
        out_shape=out_shape,
        scratch_shapes=[pltpu.VMEM((k, d), BF16)],
        compiler_params=_params("arbitrary"),
        name="out_proj_norm",
    )(a, w, residual, norm_w)
    return (outs[0], outs[1]) if emit_x else (None, outs[0])


def kernel(x, norm_w, w_in, gmlp_ln_w, gmlp_ln_b, gmlp_ws, gmlp_bs, conv_w, conv_b, dt_bias,
           a_log, d_skip, ssm_norm_w, w_branch_a, w_branch_b, w_branch_c, w_out, final_norm_w):
    b, s, d = x.shape
    assert (s, d) == (SEQ, D_MODEL)
    cos2, sin2 = _rope_tables()
    rows = lambda p: p[:, None, :]
    norm_w3, ln_w3, ln_b3 = rows(norm_w), rows(gmlp_ln_w), rows(gmlp_ln_b)
    conv_b3, ssm_norm_w3 = rows(conv_b), rows(ssm_norm_w)
    bs_t = jnp.swapaxes(gmlp_bs, 1, 2)
    dt_bias3, a_log3 = _pad_lanes(dt_bias), _pad_lanes(a_log)
    d_skip_e = rows(jnp.repeat(d_skip, SSM_HEAD_DIM, axis=1))
    wt_in = jnp.swapaxes(w_in, 1, 2)
    next_norm_w3 = jnp.concatenate([norm_w3[1:], final_norm_w.reshape(1, 1, d)], axis=0)
    xf = x.reshape(b * s, d)
    tm, tn = PROJ_TILE_M, PROJ_TILE_N
    h = _rmsnorm(xf, norm_w3, 0)
    for i in range(DEPTH):
        last = i == DEPTH - 1
        proj = _matmul_nt(h, wt_in, i, OFF_A, OFF_C + WIDTH_C - OFF_A, tm, tn, name="in_proj")
        ya = _gmlp(proj, ln_w3, ln_b3, gmlp_ws, bs_t, i)
        yb = _attention(proj, cos2, sin2)
        yc = _ssd(proj, conv_w, conv_b3, dt_bias3, a_log3, d_skip_e, ssm_norm_w3, i)
        merged = _merge(h, ya, yb, yc, wt_in, w_branch_a, w_branch_b, w_branch_c, i)
        xf, h = _out_proj_norm(merged, w_out, xf, next_norm_w3, i, x.dtype if last else BF16)
    return h.reshape(b, s, d)
```

```python
import math

import jax
import jax.numpy as jnp
from jax import lax
from jax.experimental import pallas as pl
from jax.experimental.pallas import tpu as pltpu

D_MODEL = 2048
SEQ = 2048
DEPTH = 4
BRANCH_WIDTH = 1536
GMLP_CHUNK = 128
GMLP_GROUPS = 12
HEAD_DIM = 128
HEADS_PER_GROUP = 4
DILATED_GROUPS = ((128, 1), (512, 4), (2048, 16))
ATT_BLOCK = 128
ATT_OUT_WIDTH = HEADS_PER_GROUP * HEAD_DIM
ROPE_THETA = 10000.0
SSM_D_INNER = 1536
SSM_HEAD_DIM = 64
SSM_HEADS = 24
SSM_GROUPS = 4
SSM_D_STATE = 128
SSM_CONV = 4
SSM_CHUNK = 128
SSM_CONV_DIM = SSM_D_INNER + 2 * SSM_GROUPS * SSM_D_STATE
HEAD_PAIRS = SSM_HEADS // 2
PAIRS_PER_GROUP = HEAD_PAIRS // SSM_GROUPS

OFF_A = 0
OFF_B = 3 * BRANCH_WIDTH
OFF_C = OFF_B + 3 * BRANCH_WIDTH + ATT_OUT_WIDTH
OFF_DT = OFF_C + SSM_D_INNER + SSM_CONV_DIM
OFF_GATE = OFF_DT + SSM_HEADS
WIDTH_C = 4608
DT_COL = SSM_D_INNER + SSM_CONV_DIM

LANES = 128
SUBLANES = 8
LOG2_E = 1.4426950408889634
LN_2 = 0.6931471805599453
VMEM_LIMIT = 56 * 1024 * 1024
PROJ_TILE_M, PROJ_TILE_N = 2048, 1024
OUT_TILE_M = 512
MERGE_TILE_M, MERGE_TILE_N = 1024, 256

F32 = jnp.float32
BF16 = jnp.bfloat16


def _params(*semantics):
    return pltpu.CompilerParams(dimension_semantics=semantics, vmem_limit_bytes=VMEM_LIMIT)


def _layer_rows(layer, width, n_grid):
    if n_grid == 1:
        return pl.BlockSpec((None, 1, width), lambda i: (layer, 0, 0))
    return pl.BlockSpec((None, 1, width), lambda i, j: (layer, 0, 0))


def _rmsnorm_kernel(x_ref, w_ref, o_ref):
    x = x_ref[...]
    ms = jnp.mean(x * x, axis=-1, keepdims=True)
    o_ref[...] = (x * lax.rsqrt(ms + 1e-6) * w_ref[...]).astype(o_ref.dtype)


def _rmsnorm(x, w, layer, rows=256):
    m, d = x.shape
    return pl.pallas_call(
        _rmsnorm_kernel,
        grid=(m // rows,),
        in_specs=[pl.BlockSpec((rows, d), lambda i: (i, 0)), _layer_rows(layer, d, 1)],
        out_specs=pl.BlockSpec((rows, d), lambda i: (i, 0)),
        out_shape=jax.ShapeDtypeStruct((m, d), BF16),
        compiler_params=_params("parallel"),
        name="rmsnorm",
    )(x, w)


NT_DIMS = (((1,), (1,)), ((), ()))


def _wt_rows_spec(layer, row_off, tn, k):
    assert row_off % SUBLANES == 0 and tn % SUBLANES == 0
    return pl.BlockSpec((pl.Element(1), pl.Element(tn), pl.Element(k)),
                        lambda i, j: (layer, pl.multiple_of(row_off + j * tn, SUBLANES), 0))


def _matmul_nt_kernel(a_ref, wt_ref, o_ref):
    o_ref[...] = lax.dot_general(a_ref[...], wt_ref[0].astype(BF16), NT_DIMS,
                                 preferred_element_type=F32)


def _matmul_nt(a, wt, layer, row_off, n_cols, tm, tn, name):
    m, k = a.shape
    assert n_cols % tn == 0 and m % tm == 0
    return pl.pallas_call(
        _matmul_nt_kernel,
        grid=(m // tm, n_cols // tn),
        in_specs=[pl.BlockSpec((tm, k), lambda i, j: (i, 0), pipeline_mode=pl.Buffered(1)),
                  _wt_rows_spec(layer, row_off, tn, k)],
        out_specs=pl.BlockSpec((tm, tn), lambda i, j: (i, j)),
        out_shape=jax.ShapeDtypeStruct((m, n_cols), F32),
        compiler_params=_params("parallel", "arbitrary"),
        name=name,
    )(a, wt)


GMLP_STEP_ROWS = 4 * GMLP_CHUNK

GELU_K1 = -2.0 * math.sqrt(2.0 / math.pi) * LOG2_E
GELU_K3 = 0.044715 * GELU_K1


def _gelu_tanh(x):
    return x / (1.0 + jnp.exp2(x * (GELU_K1 + GELU_K3 * (x * x))))


def _gmlp_kernel(u_ref, v_ref, z_ref, lnw_ref, lnb_ref, ws_ref, bst_ref, o_ref, wm_ref, vc_s):
    @pl.when(pl.program_id(0) == 0)
    def _():
        row = lax.broadcasted_iota(jnp.int32, (GMLP_CHUNK, GMLP_CHUNK), 0)
        col = lax.broadcasted_iota(jnp.int32, (GMLP_CHUNK, GMLP_CHUNK), 1)
        for g in range(GMLP_GROUPS):
            wm_ref[g] = jnp.where(col <= row, ws_ref[g], 0.0).astype(BF16)

    for c in range(GMLP_STEP_ROWS // GMLP_CHUNK):
        rows = slice(c * GMLP_CHUNK, (c + 1) * GMLP_CHUNK)
        v = _gelu_tanh(v_ref[rows, :])
        mu = jnp.mean(v, axis=-1, keepdims=True)
        vc = v - mu
        var = jnp.mean(vc * vc, axis=-1, keepdims=True)
        rs = lax.rsqrt(var + 1e-5)
        vc_s[...] = vc
        for g in range(GMLP_GROUPS):
            sl = slice(g * LANES, (g + 1) * LANES)
            vn = vc_s[:, sl] * rs * lnw_ref[:, sl] + lnb_ref[:, sl]
            mixed = jnp.dot(wm_ref[g], vn.astype(BF16), preferred_element_type=F32)
            mixed = mixed + bst_ref[:, g:g + 1]
            u = _gelu_tanh(u_ref[rows, sl])
            o_ref[rows, sl] = (u * mixed * jax.nn.silu(z_ref[rows, sl])).astype(o_ref.dtype)


def _gmlp(pa, ln_w, ln_b, ws, bs_t, layer):
    m = pa.shape[0]
    w = BRANCH_WIDTH
    return pl.pallas_call(
        _gmlp_kernel,
        grid=(m // GMLP_STEP_ROWS,),
        in_specs=[pl.BlockSpec((GMLP_STEP_ROWS, w), lambda i: (i, 0)),
                  pl.BlockSpec((GMLP_STEP_ROWS, w), lambda i: (i, 1)),
                  pl.BlockSpec((GMLP_STEP_ROWS, w), lambda i: (i, 2)),
                  _layer_rows(layer, w, 1),
                  _layer_rows(layer, w, 1),
                  pl.BlockSpec((None, GMLP_GROUPS, GMLP_CHUNK, GMLP_CHUNK),
                               lambda i: (layer, 0, 0, 0)),
                  pl.BlockSpec((None, GMLP_CHUNK, GMLP_GROUPS), lambda i: (layer, 0, 0))],
        out_specs=pl.BlockSpec((GMLP_STEP_ROWS, w), lambda i: (i, 0)),
        out_shape=jax.ShapeDtypeStruct((m, w), BF16),
        scratch_shapes=[pltpu.VMEM((GMLP_GROUPS, GMLP_CHUNK, GMLP_CHUNK), BF16),
                        pltpu.VMEM((GMLP_CHUNK, BRANCH_WIDTH), F32)],
        compiler_params=_params("arbitrary"),
        name="gmlp",
    )(pa, pa, pa, ln_w, ln_b, ws, bs_t)


ROPE_ROWS = 256
ATT_UNROLL = 16


def _attn_kernel(q1, q2, q3, k1, k2, k3, v1, v2, v3, z_ref, cos_ref, sin_ref, o_ref,
                 qs1, qs2, qs3, ks1, ks2, ks3, os1, os2, os3, ls1, ls2, ls3,
                 sc1, sc2, sc3, mx1, mx2, mx3):
    scale = LOG2_E / math.sqrt(HEAD_DIM)

    def rope_step(i, carry):
        rows = pl.ds(pl.multiple_of(i * ROPE_ROWS, ROPE_ROWS), ROPE_ROWS)
        c = cos_ref[rows, :]
        s = sin_ref[rows, :]
        for src, dst, mul in ((q1, qs1, scale), (q2, qs2, scale), (q3, qs3, scale),
                              (k1, ks1, None), (k2, ks2, None), (k3, ks3, None)):
            t = src[rows, :]
            partner = pltpu.bitcast((t * s).astype(BF16), jnp.uint32)
            partner = pltpu.bitcast(pltpu.roll(partner, HEAD_DIM // 2, 1), BF16)
            r = t * c + partner.astype(F32)
            dst[rows, :] = r if mul is None else r * mul
        return carry

    for i in range(SEQ // ROPE_ROWS):
        rope_step(i, 0)

    row = lax.broadcasted_iota(jnp.int32, (ATT_BLOCK, ATT_BLOCK), 0)
    col = lax.broadcasted_iota(jnp.int32, (ATT_BLOCK, ATT_BLOCK), 1)
    lower = col <= row
    upper = col >= row
    ones = jnp.ones((ATT_BLOCK, HEAD_DIM), BF16)

    def with_ones(v):
        return jnp.concatenate([v.astype(BF16), ones], axis=1)

    def block_rows(dil, idx):
        def rows_at(start):
            if dil == 1:
                return pl.ds(pl.multiple_of(start, ATT_BLOCK), ATT_BLOCK)
            return pl.ds(start, ATT_BLOCK, stride=dil)

        r = idx % dil
        bi = idx // dil
        cur = rows_at(bi * (ATT_BLOCK * dil) + r)
        prev = rows_at(jnp.maximum(bi - 1, 0) * (ATT_BLOCK * dil) + r)
        return cur, prev, bi > 0

    def scores(qs, ks, sc, mx, dil, idx):
        cur, prev, has_prev = block_rows(dil, idx)
        q = qs[cur, :].astype(BF16)
        s_cur = lax.dot_general(q, ks[cur, :].astype(BF16), NT_DIMS,
                                preferred_element_type=F32)
        s_cur = jnp.where(lower, s_cur, -jnp.inf)
        sc[idx, :, 0:ATT_BLOCK] = s_cur
        if SEQ // dil > ATT_BLOCK:
            s_prev = lax.dot_general(q, ks[prev, :].astype(BF16), NT_DIMS,
                                     preferred_element_type=F32)
            s_prev = jnp.where(jnp.logical_and(upper, has_prev), s_prev, -jnp.inf)
            sc[idx, :, ATT_BLOCK:2 * ATT_BLOCK] = s_prev
            s_cur = jnp.maximum(s_cur, s_prev)
        mx[idx] = jnp.broadcast_to(jnp.max(s_cur, axis=-1, keepdims=True), (ATT_BLOCK, HEAD_DIM))

    def values(v_ref, sc, mx, os, ls, dil, idx):
        cur, prev, _ = block_rows(dil, idx)
        m = mx[idx]
        acc = jnp.dot(jnp.exp2(sc[idx, :, 0:ATT_BLOCK] - m).astype(BF16),
                      with_ones(v_ref[cur, :]), preferred_element_type=F32)
        if SEQ // dil > ATT_BLOCK:
            acc = acc + jnp.dot(jnp.exp2(sc[idx, :, ATT_BLOCK:2 * ATT_BLOCK] - m).astype(BF16),
                                with_ones(v_ref[prev, :]), preferred_element_type=F32)
        l = acc[:, HEAD_DIM:]
        os[cur, :] = acc[:, :HEAD_DIM] / l
        ls[cur, :] = m * LN_2 + jnp.log(l)

    groups = ((qs1, ks1, v1, os1, ls1, sc1, mx1, DILATED_GROUPS[0][1]),
              (qs2, ks2, v2, os2, ls2, sc2, mx2, DILATED_GROUPS[1][1]),
              (qs3, ks3, v3, os3, ls3, sc3, mx3, DILATED_GROUPS[2][1]))
    n_idx = SEQ // ATT_BLOCK

    def scores_step(it, carry):
        for u in range(ATT_UNROLL):
            for qs, ks, _, _, _, sc, mx, dil in groups:
                scores(qs, ks, sc, mx, dil, it * ATT_UNROLL + u)
        return carry

    def values_step(it, carry):
        for u in range(ATT_UNROLL):
            for _, _, v_ref, os, ls, sc, mx, dil in groups:
                values(v_ref, sc, mx, os, ls, dil, it * ATT_UNROLL + u)
        return carry

    lax.fori_loop(0, n_idx // ATT_UNROLL, scores_step, 0)
    lax.fori_loop(0, n_idx // ATT_UNROLL, values_step, 0)

    def combine_step(i, carry):
        rows = pl.ds(pl.multiple_of(i * ROPE_ROWS, ROPE_ROWS), ROPE_ROWS)
        l1 = ls1[rows, :]
        l2 = ls2[rows, :]
        l3 = ls3[rows, :]
        mx = jnp.maximum(jnp.maximum(l1, l2), l3)
        e1 = jnp.exp(l1 - mx)
        e2 = jnp.exp(l2 - mx)
        e3 = jnp.exp(l3 - mx)
        o = (e1 * os1[rows, :] + e2 * os2[rows, :] + e3 * os3[rows, :]) / (e1 + e2 + e3)
        o_ref[rows, :] = (o * jax.nn.silu(z_ref[rows, :])).astype(o_ref.dtype)
        return carry

    lax.fori_loop(0, SEQ // ROPE_ROWS, combine_step, 0)


def _rope_tables():
    inv = 1.0 / (ROPE_THETA ** (jnp.arange(0, HEAD_DIM, 2, dtype=F32) / HEAD_DIM))
    ang = jnp.arange(SEQ, dtype=F32)[:, None] * inv[None, :]
    cos, sin = jnp.cos(ang), jnp.sin(ang)
    return jnp.concatenate([cos, cos], axis=-1), jnp.concatenate([sin, -sin], axis=-1)


def _attention(pb, cos2, sin2):
    m = pb.shape[0]
    batch = m // SEQ
    hpg = HEADS_PER_GROUP
    n_qkv = 3 * hpg
    n_blocks = SEQ // ATT_BLOCK

    first = OFF_B // HEAD_DIM

    def head_spec(seg, grp):
        return pl.BlockSpec((SEQ, HEAD_DIM),
                            lambda b, j: (b, first + seg * n_qkv + grp * hpg + j))

    in_specs = [head_spec(seg, grp) for seg in range(3) for grp in range(3)]
    in_specs.append(pl.BlockSpec((SEQ, HEAD_DIM), lambda b, j: (b, first + 3 * n_qkv + j)))
    in_specs += [pl.BlockSpec((SEQ, HEAD_DIM), lambda b, j: (0, 0))] * 2
    return pl.pallas_call(
        _attn_kernel,
        grid=(batch, hpg),
        in_specs=in_specs,
        out_specs=pl.BlockSpec((SEQ, HEAD_DIM), lambda b, j: (b, j)),
        out_shape=jax.ShapeDtypeStruct((m, ATT_OUT_WIDTH), BF16),
        scratch_shapes=([pltpu.VMEM((SEQ, HEAD_DIM), F32)] * 12
                        + [pltpu.VMEM((n_blocks, ATT_BLOCK, 2 * ATT_BLOCK), F32)] * 2
                        + [pltpu.VMEM((n_blocks, ATT_BLOCK, ATT_BLOCK), F32)] * 4),
        compiler_params=_params("parallel", "parallel"),
        name="dilated_attention",
    )(*([pb] * 10), cos2, sin2)


CONV_HALO = 8
CONV_TILES = SSM_CONV_DIM // LANES


def _ssd_kernel(p_ref, convw_ref, convb_ref, dtb_ref, alog_ref, dskip_ref, normw_ref,
                o_ref, xext, xcs, ys, hstate):
    c = pl.program_id(1)
    L = SSM_CHUNK

    @pl.when(c == 0)
    def _():
        hstate[...] = jnp.zeros_like(hstate)
        xext[:, 0:CONV_HALO, :] = jnp.zeros((CONV_TILES, CONV_HALO, LANES), F32)

    for t in range(CONV_TILES):
        sl = slice(t * LANES, (t + 1) * LANES)
        xext[t, CONV_HALO:CONV_HALO + L, :] = p_ref[:, SSM_D_INNER + t * LANES:
                                                    SSM_D_INNER + (t + 1) * LANES]
        acc = jnp.broadcast_to(convb_ref[:, sl], (L, LANES))
        for k in range(SSM_CONV):
            lo = CONV_HALO - (SSM_CONV - 1) + k
            acc = acc + convw_ref[k:k + 1, sl] * xext[t, lo:lo + L, :]
        xcs[:, sl] = jax.nn.silu(acc)
        xext[t, 0:CONV_HALO, :] = xext[t, L:L + CONV_HALO, :]

    lane = lax.broadcasted_iota(jnp.int32, (L, LANES), 1)
    row = lax.broadcasted_iota(jnp.int32, (L, L), 0)
    col = lax.broadcasted_iota(jnp.int32, (L, L), 1)
    causal = col <= row
    first_head = lane < SSM_HEAD_DIM

    dt = jnp.where(lane < SSM_HEADS,
                   jax.nn.softplus(p_ref[:, DT_COL:DT_COL + LANES] + dtb_ref[...]), 0.0)
    da = dt * (-jnp.exp(alog_ref[...]))
    cum = jnp.dot(causal.astype(F32), da, precision=lax.Precision.HIGHEST,
                  preferred_element_type=F32)
    cum = cum * LOG2_E
    cum_t = cum.T
    dt_t = dt.T
    dsdt_t = jnp.exp2(cum_t[:, L - 1:L] - cum_t) * dt_t
    shift_t = cum_t - jnp.log2(dt_t)

    for g in range(SSM_GROUPS):
        b_off = SSM_D_INNER + g * SSM_D_STATE
        c_off = SSM_D_INNER + SSM_GROUPS * SSM_D_STATE + g * SSM_D_STATE
        bm_t = xcs[:, b_off:b_off + SSM_D_STATE].T
        cm = xcs[:, c_off:c_off + SSM_D_STATE].astype(BF16)
        cb = jnp.dot(cm, bm_t.astype(BF16), preferred_element_type=F32)
        for pj in range(PAIRS_PER_GROUP):
            pair = g * PAIRS_PER_GROUP + pj
            sl = slice(pair * LANES, (pair + 1) * LANES)
            xs = xcs[:, sl]
            xs_b = xs.astype(BF16)
            y_h, st_h, cc_h = [], [], []
            for hh in range(2):
                h = 2 * pair + hh
                ccol = jnp.broadcast_to(cum[:, h:h + 1], (L, L))
                decay_dt = jnp.exp2(jnp.where(causal, ccol - shift_t[h:h + 1, :], -jnp.inf))
                y_h.append(jnp.dot((cb * decay_dt).astype(BF16), xs_b,
                                   preferred_element_type=F32))
                bw = (bm_t * dsdt_t[h:h + 1, :]).astype(BF16)
                st_h.append(jnp.dot(bw, xs_b, preferred_element_type=F32))
                cc_h.append(ccol)
            expcum = jnp.exp2(jnp.where(first_head, cc_h[0], cc_h[1]))
            h_prev = hstate[pair]
            y = jnp.where(first_head, y_h[0], y_h[1])
            y = y + jnp.dot(cm, h_prev.astype(BF16), preferred_element_type=F32) * expcum
            ys[:, sl] = y + dskip_ref[:, sl] * xs
            hstate[pair] = (h_prev * expcum[L - 1:L, :]
                            + jnp.where(first_head, st_h[0], st_h[1]))

    gw = SSM_D_INNER // SSM_GROUPS
    for g in range(SSM_GROUPS):
        sl = slice(g * gw, (g + 1) * gw)
        y = ys[:, sl] * jax.nn.silu(p_ref[:, sl])
        ms = jnp.mean(y * y, axis=-1, keepdims=True)
        o_ref[:, sl] = (y * lax.rsqrt(ms + 1e-5) * normw_ref[:, sl]).astype(o_ref.dtype)


def _pad_lanes(v):
    return jnp.pad(v, ((0, 0), (0, LANES - v.shape[1])))[:, None, :]


def _ssd(pc, conv_w, conv_b, dt_bias, a_log, d_skip_e, norm_w, layer):
    m = pc.shape[0]
    batch = m // SEQ
    nc = SEQ // SSM_CHUNK
    return pl.pallas_call(
        _ssd_kernel,
        grid=(batch, nc),
        in_specs=[pl.BlockSpec((pl.Element(SSM_CHUNK), pl.Element(WIDTH_C)),
                               lambda b, c: (pl.multiple_of((b * nc + c) * SSM_CHUNK, SSM_CHUNK),
                                             OFF_C)),
                  pl.BlockSpec((None, SSM_CONV, SSM_CONV_DIM), lambda b, c: (layer, 0, 0)),
                  _layer_rows(layer, SSM_CONV_DIM, 2),
                  _layer_rows(layer, LANES, 2),
                  _layer_rows(layer, LANES, 2),
                  _layer_rows(layer, SSM_D_INNER, 2),
                  _layer_rows(layer, SSM_D_INNER, 2)],
        out_specs=pl.BlockSpec((SSM_CHUNK, SSM_D_INNER), lambda b, c: (b * nc + c, 0)),
        out_shape=jax.ShapeDtypeStruct((m, SSM_D_INNER), BF16),
        scratch_shapes=[pltpu.VMEM((CONV_TILES, CONV_HALO + SSM_CHUNK, LANES), F32),
                        pltpu.VMEM((SSM_CHUNK, SSM_CONV_DIM), F32),
                        pltpu.VMEM((SSM_CHUNK, SSM_D_INNER), F32),
                        pltpu.VMEM((HEAD_PAIRS, SSM_D_STATE, LANES), F32)],
        compiler_params=_params("parallel", "arbitrary"),
        name="ssd",
    )(pc, conv_w, conv_b, dt_bias, a_log, d_skip_e, norm_w)


def _merge_kernel(h_ref, ya_ref, yb_ref, yc_ref, ga_ref, gb_ref, gc_ref, wa_ref, wb_ref, wc_ref,
                  o_ref):
    h = h_ref[...]

    def branch(y_ref, g_ref, w_ref):
        gate = lax.dot_general(h, g_ref[0].astype(BF16), NT_DIMS, preferred_element_type=F32)
        return jax.nn.sigmoid(gate) * jnp.dot(y_ref[...], w_ref[...].astype(BF16),
                                              preferred_element_type=F32)
    acc = branch(ya_ref, ga_ref, wa_ref) + branch(yb_ref, gb_ref, wb_ref)
    o_ref[...] = (acc + branch(yc_ref, gc_ref, wc_ref)).astype(o_ref.dtype)


def _merge(h, ya, yb, yc, wt_in, wa, wb, wc, layer):
    m = ya.shape[0]
    tm, tn = MERGE_TILE_M, MERGE_TILE_N
    y_spec = lambda w: pl.BlockSpec((tm, w), lambda i, j: (i, 0))
    g_spec = lambda k: _wt_rows_spec(layer, OFF_GATE + k * D_MODEL, tn, D_MODEL)
    w_spec = lambda w: pl.BlockSpec((None, w, tn), lambda i, j: (layer, 0, j))
    return pl.pallas_call(
        _merge_kernel,
        grid=(m // tm, D_MODEL // tn),
        in_specs=[y_spec(D_MODEL), y_spec(BRANCH_WIDTH), y_spec(ATT_OUT_WIDTH),
                  y_spec(SSM_D_INNER), g_spec(0), g_spec(1), g_spec(2),
                  w_spec(BRANCH_WIDTH), w_spec(ATT_OUT_WIDTH), w_spec(SSM_D_INNER)],
        out_specs=pl.BlockSpec((tm, tn), lambda i, j: (i, j)),
        out_shape=jax.ShapeDtypeStruct((m, D_MODEL), BF16),
        compiler_params=_params("parallel", "arbitrary"),
        name="merge",
    )(h, ya, yb, yc, wt_in, wt_in, wt_in, wa, wb, wc)


def _out_proj_norm_kernel(a_ref, w_ref, r_ref, nw_ref, *outs_and_scratch):
    *x_out, h_ref, w_bf = outs_and_scratch

    @pl.when(pl.program_id(0) == 0)
    def _():
        w_bf[...] = w_ref[...].astype(BF16)

    x_new = r_ref[...] + jnp.dot(a_ref[...], w_bf[...], preferred_element_type=F32)
    for x_ref in x_out:
        x_ref[...] = x_new
    ms = jnp.mean(x_new * x_new, axis=-1, keepdims=True)
    h_ref[...] = (x_new * lax.rsqrt(ms + 1e-6) * nw_ref[...]).astype(h_ref.dtype)


def _out_proj_norm(a, w, residual, norm_w, layer, h_dtype):
    m, k = a.shape
    d = D_MODEL
    tm = OUT_TILE_M
    emit_x = h_dtype == BF16
    row_spec = pl.BlockSpec((tm, d), lambda i: (i, 0))
    out_specs = [row_spec] * (2 if emit_x else 1)
    out_shape = [jax.ShapeDtypeStruct((m, d), F32)] * emit_x + [jax.ShapeDtypeStruct((m, d), h_dtype)]
    outs = pl.pallas_call(
        _out_proj_norm_kernel,
        grid=(m // tm,),
        in_specs=[pl.BlockSpec((tm, k), lambda i: (i, 0)),
                  pl.BlockSpec((None, k, d), lambda i: (layer, 0, 0),
                               pipeline_mode=pl.Buffered(1)),
                  row_spec,
                  _layer_rows(layer, d, 1)],
        out_specs=out_specs,
        out_shape=out_shape,
        scratch_shapes=[pltpu.VMEM((k, d), BF16)],
        compiler_params=_params("arbitrary"),
        name="out_proj_norm",
    )(a, w, residual, norm_w)
    return (outs[0], outs[1]) if emit_x else (None, outs[0])


def kernel(x, norm_w, w_in, gmlp_ln_w, gmlp_ln_b, gmlp_ws, gmlp_bs, conv_w, conv_b, dt_bias,
           a_log, d_skip, ssm_norm_w, w_branch_a, w_branch_b, w_branch_c, w_out, final_norm_w):
    b, s, d = x.shape
    assert (s, d) == (SEQ, D_MODEL)
    cos2, sin2 = _rope_tables()
    rows = lambda p: p[:, None, :]
    norm_w3, ln_w3, ln_b3 = rows(norm_w), rows(gmlp_ln_w), rows(gmlp_ln_b)
    conv_b3, ssm_norm_w3 = rows(conv_b), rows(ssm_norm_w)
    bs_t = jnp.swapaxes(gmlp_bs, 1, 2)
    dt_bias3, a_log3 = _pad_lanes(dt_bias), _pad_lanes(a_log)
    d_skip_e = rows(jnp.repeat(d_skip, SSM_HEAD_DIM, axis=1))
    wt_in = jnp.swapaxes(w_in, 1, 2)
    next_norm_w3 = jnp.concatenate([norm_w3[1:], final_norm_w.reshape(1, 1, d)], axis=0)
    xf = x.reshape(b * s, d)
    tm, tn = PROJ_TILE_M, PROJ_TILE_N
    h = _rmsnorm(xf, norm_w3, 0)
    for i in range(DEPTH):
        last = i == DEPTH - 1
        proj = _matmul_nt(h, wt_in, i, OFF_A, OFF_C + WIDTH_C - OFF_A, tm, tn, name="in_proj")
        ya = _gmlp(proj, ln_w3, ln_b3, gmlp_ws, bs_t, i)
        yb = _attention(proj, cos2, sin2)
        yc = _ssd(proj, conv_w, conv_b3, dt_bias3, a_log3, d_skip_e, ssm_norm_w3, i)
        merged = _merge(h, ya, yb, yc, wt_in, w_branch_a, w_branch_b, w_branch_c, i)
        xf, h = _out_proj_norm(merged, w_out, xf, next_norm_w3, i, x.dtype if last else BF16)
    return h.reshape(b, s, d)
```
